```python
import math
import jax
import jax.numpy as jnp
from jax import lax
import numpy as np

D_MODEL = 2048
BATCH = 2
SEQ = 4096
DEPTH = 1
DEC_BATCH = 32
DEC_SEQ = 1
PAST_LEN = 8192
PAGE_SIZE = 128

D_ATTN = D_MODEL // 2
HEAD_DIM = 128
N_HEADS = D_ATTN // HEAD_DIM
ROT_DIM = HEAD_DIM // 4
ROPE_THETA = 500000.0
WINDOWS = (128, 512, 2048)
DILATIONS = (1, 4, 16)
MAX_WINDOW = max(WINDOWS)
D_SSM = D_MODEL - D_ATTN
SSM_CH = 16
SSM_GROUPS = D_SSM // SSM_CH
SSM_STATE = 64
DT_MIN = 0.001
DT_MAX = 0.1
N_EXPERTS = 256
TOP_K = 8
N_EXPERT_GROUPS = 8
TOPK_GROUPS = 4
D_EXPERT = D_MODEL // 4
ROUTE_SCALE = 2.5
EPS = 1e-6
D_IN_PROJ = 3 * D_ATTN + D_SSM
F32 = jnp.float32

kernel_name = 'hymba_dilated_s5_moe_step'


def _rms(x, g):
    xf = x.astype(F32)
    return xf * lax.rsqrt(jnp.mean(xf * xf, axis=-1, keepdims=True) + EPS) * g.astype(F32)


def _rope(x, pos):
    half = ROT_DIM // 2
    inv = jnp.exp(-math.log(ROPE_THETA) * jnp.arange(half, dtype=F32) / half)
    ang = pos.astype(F32)[:, None] * inv[None, :]
    cos = jnp.cos(ang)[None, :, None, :]
    sin = jnp.sin(ang)[None, :, None, :]
    x1, x2, rest = x[..., :half], x[..., half:ROT_DIM], x[..., ROT_DIM:]
    return jnp.concatenate([x1 * cos - x2 * sin, x1 * sin + x2 * cos, rest], axis=-1)


def _dilated_prompt(q, k, v, dil, span):
    B, S, H, Dh = q.shape
    blk = span
    unit = dil * blk
    s_pad = -(-S // unit) * unit
    nb = s_pad // unit

    def split(a):
        a = jnp.pad(a, ((0, 0), (0, s_pad - S), (0, 0), (0, 0)))
        return a.reshape(B, nb, blk, dil, H, Dh)

    def with_prev(a):
        prev = jnp.pad(a[:, :-1], ((0, 0), (1, 0), (0, 0), (0, 0), (0, 0), (0, 0)))
        return jnp.concatenate([prev, a], axis=2)

    qs = split(q)
    kk = with_prev(split(k))
    vv = with_prev(split(v))
    s = jnp.einsum('bnqrhd,bnkrhd->bnrhqk', qs, kk) / math.sqrt(Dh)
    qi = jnp.arange(blk)[:, None]
    ki = jnp.arange(2 * blk)[None, :]
    dist = blk + qi - ki
    band = (dist >= 0) & (dist <= span)
    valid = band[None] & ((jnp.arange(nb)[:, None, None] > 0) | (ki >= blk)[None])
    s = jnp.where(valid[None, :, None, None], s, -jnp.inf)
    m = jnp.max(s, axis=-1, keepdims=True)
    p = jnp.exp(s - m)
    den = jnp.sum(p, axis=-1, keepdims=True)
    o = jnp.einsum('bnrhqk,bnkrhd->bnrhqd', p, vv) / den
    lse = (m + jnp.log(den))[..., 0]
    o = o.transpose(0, 1, 4, 2, 3, 5).reshape(B, s_pad, H, Dh)[:, :S]
    lse = lse.transpose(0, 1, 4, 2, 3).reshape(B, s_pad, H)[:, :S]
    return o, lse


def _dilated_sample(q, k_all, v_all, n_buf, dil, span):
    T = q.shape[1]
    idx = n_buf + jnp.arange(T)[:, None] - dil * jnp.arange(span + 1)[None, :]
    valid = idx >= 0
    idx = jnp.maximum(idx, 0)
    kg = jnp.take(k_all, idx, axis=1)
    vg = jnp.take(v_all, idx, axis=1)
    s = jnp.einsum('bthd,btjhd->bthj', q, kg) / math.sqrt(q.shape[-1])
    s = jnp.where(valid[None, :, None, :], s, -jnp.inf)
    m = jnp.max(s, axis=-1, keepdims=True)
    p = jnp.exp(s - m)
    den = jnp.sum(p, axis=-1, keepdims=True)
    o = jnp.einsum('bthj,btjhd->bthd', p, vg) / den
    return o, (m + jnp.log(den))[..., 0]


def _combine(outs, lses):
    w = jax.nn.softmax(jnp.stack(lses, axis=0), axis=0)[..., None]
    return jnp.sum(w * jnp.stack(outs, axis=0), axis=0)


def _s5(u, h0_re, h0_im, p):
    Bsz, S, _ = u.shape
    ug = u.reshape(Bsz, S, SSM_GROUPS, SSM_CH)
    ar = p['ssm_A_re'].astype(F32)
    ai = p['ssm_A_im'].astype(F32)
    dt = jnp.exp(p['ssm_log_dt'].astype(F32))[:, None]
    mag = jnp.exp(dt * ar)
    abar_re = mag * jnp.cos(dt * ai)
    abar_im = mag * jnp.sin(dt * ai)
    nr = abar_re - 1.0
    den = ar * ar + ai * ai
    coef_re = (nr * ar + abar_im * ai) / den
    coef_im = (abar_im * ar - nr * ai) / den
    b_re = p['ssm_B_re'].astype(F32)
    b_im = p['ssm_B_im'].astype(F32)
    bbar_re = coef_re[..., None] * b_re - coef_im[..., None] * b_im
    bbar_im = coef_re[..., None] * b_im + coef_im[..., None] * b_re
    bu_re = jnp.einsum('gpc,bsgc->bsgp', bbar_re, ug)
    bu_im = jnp.einsum('gpc,bsgc->bsgp', bbar_im, ug)
    bu_re = bu_re.at[:, 0].add(abar_re * h0_re - abar_im * h0_im)
    bu_im = bu_im.at[:, 0].add(abar_re * h0_im + abar_im * h0_re)
    a_re = jnp.broadcast_to(abar_re, bu_re.shape)
    a_im = jnp.broadcast_to(abar_im, bu_im.shape)

    def comb(e1, e2):
        a1r, a1i, b1r, b1i = e1
        a2r, a2i, b2r, b2i = e2
        return (a2r * a1r - a2i * a1i, a2r * a1i + a2i * a1r,
                a2r * b1r - a2i * b1i + b2r, a2r * b1i + a2i * b1r + b2i)

    _, _, h_re, h_im = lax.associative_scan(comb, (a_re, a_im, bu_re, bu_im), axis=1)
    y = (jnp.einsum('gcp,bsgp->bsgc', p['ssm_C_re'].astype(F32), h_re)
         - jnp.einsum('gcp,bsgp->bsgc', p['ssm_C_im'].astype(F32), h_im))
    y = y.reshape(Bsz, S, D_SSM) + p['ssm_D'].astype(F32) * u
    g = jax.nn.gelu(y)
    out = g * jax.nn.sigmoid(g @ p['ssm_w_glu'].astype(F32) + p['ssm_b_glu'].astype(F32))
    return out, h_re[:, -1], h_im[:, -1]


def _routed_experts(t, top_e, gate, w_gate, w_up, w_down):
    T, D = t.shape
    E = w_gate.shape[0]
    K = top_e.shape[1]
    A = T * K
    blk = int(max(8, min(128, 2 ** math.ceil(math.log2(max(1, -(-A // E)))))))
    nb = -(-A // blk) + E
    e_flat = top_e.reshape(-1)
    tok_flat = jnp.arange(A, dtype=jnp.int32) // K
    w_flat = gate.reshape(-1)
    order = jnp.argsort(e_flat)
    e_s, tok_s, w_s = e_flat[order], tok_flat[order], w_flat[order]
    counts = jax.ops.segment_sum(jnp.ones((A,), jnp.int32), e_flat, num_segments=E)
    start = jnp.cumsum(counts) - counts
    padded = (counts + blk - 1) // blk * blk
    pad_end = jnp.cumsum(padded)
    pad_start = pad_end - padded
    dest = pad_start[e_s] + (jnp.arange(A, dtype=jnp.int32) - start[e_s])
    slot_tok = jnp.zeros((nb * blk,), jnp.int32).at[dest].set(tok_s)
    slot_w = jnp.zeros((nb * blk,), F32).at[dest].set(w_s)
    blk_eid = jnp.minimum(jnp.searchsorted(pad_end, jnp.arange(nb, dtype=jnp.int32) * blk, side='right'), E - 1)

    def run(args):
        toks, ws, eid = args
        xb = t[toks]
        hid = jax.nn.silu(xb @ w_gate[eid]) * (xb @ w_up[eid])
        return (hid @ w_down[eid]) * ws[:, None].astype(xb.dtype)

    out = lax.map(run, (slot_tok.reshape(nb, blk), slot_w.reshape(nb, blk), blk_eid))
    return jnp.zeros_like(t).at[slot_tok].add(out.reshape(-1, D).astype(t.dtype))


def _moe(h, p):
    B, S, D = h.shape
    t = h.reshape(B * S, D)
    T = t.shape[0]
    scores = jax.nn.sigmoid(t.astype(F32) @ p['w_router'].astype(F32))
    biased = scores + p['b_router'].astype(F32)
    per_g = N_EXPERTS // N_EXPERT_GROUPS
    g_score = lax.top_k(biased.reshape(T, N_EXPERT_GROUPS, per_g), 2)[0].sum(-1)
    _, g_idx = lax.top_k(g_score, TOPK_GROUPS)
    g_mask = jax.nn.one_hot(g_idx, N_EXPERT_GROUPS, dtype=F32).sum(1)
    e_mask = jnp.repeat(g_mask, per_g, axis=1) > 0
    _, top_e = lax.top_k(jnp.where(e_mask, biased, -jnp.inf), TOP_K)
    gate = jnp.take_along_axis(scores, top_e, axis=1)
    gate = gate / jnp.sum(gate, axis=-1, keepdims=True) * ROUTE_SCALE
    routed = _routed_experts(t, top_e, gate, p['w_exp_gate'], p['w_exp_up'], p['w_exp_down'])
    shared = (jax.nn.silu(t @ p['w_sh_gate']) * (t @ p['w_sh_up'])) @ p['w_sh_down']
    return (routed + shared).reshape(B, S, D)


def _block(x, c, pos, cache_k, cache_v, h0_re, h0_im, p):
    B, S, _ = x.shape
    dt = x.dtype
    mod = (jax.nn.silu(c.astype(F32)) @ p['w_ada'].astype(F32) + p['b_ada'].astype(F32))[:, None, :]
    sh1, sc1, g1, sh2, sc2, g2 = jnp.split(mod, 6, axis=-1)
    h = (_rms(x, p['norm1_g']) * (1.0 + sc1) + sh1).astype(dt)
    proj = h @ p['w_in']
    q = proj[..., :D_ATTN].reshape(B, S, N_HEADS, HEAD_DIM)
    k = proj[..., D_ATTN:2 * D_ATTN].reshape(B, S, N_HEADS, HEAD_DIM)
    v = proj[..., 2 * D_ATTN:3 * D_ATTN].reshape(B, S, N_HEADS, HEAD_DIM).astype(F32)
    u = proj[..., 3 * D_ATTN:].astype(F32)
    q = _rope(_rms(q, p['q_norm_g']), pos)
    k = _rope(_rms(k, p['k_norm_g']), pos)
    if cache_k is None:
        res = [_dilated_prompt(q, k, v, d, w // d) for w, d in zip(WINDOWS, DILATIONS)]
        keep = min(MAX_WINDOW, S)
        new_k, new_v = k[:, S - keep:], v[:, S - keep:]
        h0_re = jnp.zeros((B, SSM_GROUPS, SSM_STATE), F32)
        h0_im = jnp.zeros((B, SSM_GROUPS, SSM_STATE), F32)
    else:
        n_buf = cache_k.shape[1]
        k_all = jnp.concatenate([cache_k.astype(F32), k], axis=1)
        v_all = jnp.concatenate([cache_v.astype(F32), v], axis=1)
        res = [_dilated_sample(q, k_all, v_all, n_buf, d, w // d) for w, d in zip(WINDOWS, DILATIONS)]
        new_k, new_v = k_all[:, -n_buf:], v_all[:, -n_buf:]
        h0_re, h0_im = h0_re.astype(F32), h0_im.astype(F32)
    outs, lses = zip(*res)
    attn = _combine(outs, lses).reshape(B, S, D_ATTN)
    ssm, h_re, h_im = _s5(u, h0_re, h0_im, p)
    mix = jnp.concatenate([_rms(attn, p['attn_out_g']), _rms(ssm, p['ssm_out_g'])], axis=-1).astype(dt) @ p['w_out']
    x = (x.astype(F32) + g1 * mix.astype(F32)).astype(dt)
    h2 = (_rms(x, p['norm2_g']) * (1.0 + sc2) + sh2).astype(dt)
    x = (x.astype(F32) + g2 * _moe(h2, p).astype(F32)).astype(dt)
    return x, new_k.astype(dt), new_v.astype(dt), h_re, h_im


def setup_inputs(seed: int = 0) -> dict:
    key = jax.random.key(seed)
    ks = iter(list(jax.random.split(key, 48)))

    def nrm(shape, scale):
        return jax.random.normal(next(ks), shape, F32) * scale

    L, D = DEPTH, D_MODEL
    wbuf = min(MAX_WINDOW, PAST_LEN)
    inp = {}
    inp['x_prompt'] = nrm((BATCH, SEQ, D), 1.0)
    inp['x_sample'] = nrm((DEC_BATCH, DEC_SEQ, D), 1.0)
    inp['cache_k'] = nrm((L, DEC_BATCH, wbuf, N_HEADS, HEAD_DIM), 1.0)
    inp['cache_v'] = nrm((L, DEC_BATCH, wbuf, N_HEADS, HEAD_DIM), 1.0)
    inp['state_ssm_re'] = nrm((L, DEC_BATCH, SSM_GROUPS, SSM_STATE), 0.1)
    inp['state_ssm_im'] = nrm((L, DEC_BATCH, SSM_GROUPS, SSM_STATE), 0.1)
    inp['c_prompt'] = nrm((BATCH, D), 1.0)
    inp['c_sample'] = nrm((DEC_BATCH, D), 1.0)
    inp['w_ada'] = nrm((L, D, 6 * D), 0.5 * D ** -0.5)
    inp['b_ada'] = nrm((L, 6 * D), 0.02)
    inp['norm1_g'] = 1.0 + nrm((L, D), 0.02)
    inp['w_in'] = nrm((L, D, D_IN_PROJ), D ** -0.5)
    inp['q_norm_g'] = 1.0 + nrm((L, HEAD_DIM), 0.02)
    inp['k_norm_g'] = 1.0 + nrm((L, HEAD_DIM), 0.02)
    inp['ssm_A_re'] = -0.5 + nrm((L, SSM_GROUPS, SSM_STATE), 0.01)
    inp['ssm_A_im'] = jnp.pi * jnp.arange(SSM_STATE, dtype=F32) + nrm((L, SSM_GROUPS, SSM_STATE), 0.01)
    inp['ssm_log_dt'] = jax.random.uniform(next(ks), (L, SSM_GROUPS), F32, math.log(DT_MIN), math.log(DT_MAX))
    inp['ssm_B_re'] = nrm((L, SSM_GROUPS, SSM_STATE, SSM_CH), (2 * SSM_CH) ** -0.5)
    inp['ssm_B_im'] = nrm((L, SSM_GROUPS, SSM_STATE, SSM_CH), (2 * SSM_CH) ** -0.5)
    inp['ssm_C_re'] = nrm((L, SSM_GROUPS, SSM_CH, SSM_STATE), SSM_STATE ** -0.5)
    inp['ssm_C_im'] = nrm((L, SSM_GROUPS, SSM_CH, SSM_STATE), SSM_STATE ** -0.5)
    inp['ssm_D'] = nrm((L, D_SSM), 1.0)
    inp['ssm_w_glu'] = nrm((L, D_SSM, D_SSM), D_SSM ** -0.5)
    inp['ssm_b_glu'] = nrm((L, D_SSM), 0.02)
    inp['attn_out_g'] = 1.0 + nrm((L, D_ATTN), 0.02)
    inp['ssm_out_g'] = 1.0 + nrm((L, D_SSM), 0.02)
    inp['w_out'] = nrm((L, D, D), D ** -0.5)
    inp['norm2_g'] = 1.0 + nrm((L, D), 0.02)
    inp['w_router'] = nrm((L, D, N_EXPERTS), D ** -0.5)
    inp['b_router'] = nrm((L, N_EXPERTS), 0.01)
    inp['w_exp_gate'] = nrm((L, N_EXPERTS, D, D_EXPERT), D ** -0.5)
    inp['w_exp_up'] = nrm((L, N_EXPERTS, D, D_EXPERT), D ** -0.5)
    inp['w_exp_down'] = nrm((L, N_EXPERTS, D_EXPERT, D), D_EXPERT ** -0.5)
    inp['w_sh_gate'] = nrm((L, D, D_EXPERT), D ** -0.5)
    inp['w_sh_up'] = nrm((L, D, D_EXPERT), D ** -0.5)
    inp['w_sh_down'] = nrm((L, D_EXPERT, D), D_EXPERT ** -0.5)
    return inp


def reference(x_prompt, x_sample, cache_k, cache_v, state_ssm_re, state_ssm_im, c_prompt, c_sample,
              w_ada, b_ada, norm1_g, w_in, q_norm_g, k_norm_g, ssm_A_re, ssm_A_im, ssm_log_dt,
              ssm_B_re, ssm_B_im, ssm_C_re, ssm_C_im, ssm_D, ssm_w_glu, ssm_b_glu, attn_out_g,
              ssm_out_g, w_out, norm2_g, w_router, b_router, w_exp_gate, w_exp_up, w_exp_down,
              w_sh_gate, w_sh_up, w_sh_down):
    pos_p = jnp.arange(x_prompt.shape[1], dtype=jnp.int32)
    pos_s = PAST_LEN + jnp.arange(x_sample.shape[1], dtype=jnp.int32)
    y_p, y_s = x_prompt, x_sample
    kp, vp, rp, ip, ksm, vsm, rsm, ism = [], [], [], [], [], [], [], []
    for l in range(DEPTH):
        p = dict(w_ada=w_ada[l], b_ada=b_ada[l], norm1_g=norm1_g[l], w_in=w_in[l],
                 q_norm_g=q_norm_g[l], k_norm_g=k_norm_g[l], ssm_A_re=ssm_A_re[l],
                 ssm_A_im=ssm_A_im[l], ssm_log_dt=ssm_log_dt[l], ssm_B_re=ssm_B_re[l],
                 ssm_B_im=ssm_B_im[l], ssm_C_re=ssm_C_re[l], ssm_C_im=ssm_C_im[l], ssm_D=ssm_D[l],
                 ssm_w_glu=ssm_w_glu[l], ssm_b_glu=ssm_b_glu[l], attn_out_g=attn_out_g[l],
                 ssm_out_g=ssm_out_g[l], w_out=w_out[l], norm2_g=norm2_g[l], w_router=w_router[l],
                 b_router=b_router[l], w_exp_gate=w_exp_gate[l], w_exp_up=w_exp_up[l],
                 w_exp_down=w_exp_down[l], w_sh_gate=w_sh_gate[l], w_sh_up=w_sh_up[l],
                 w_sh_down=w_sh_down[l])
        y_p, k_n, v_n, r_n, i_n = _block(y_p, c_prompt, pos_p, None, None, None, None, p)
        kp.append(k_n); vp.append(v_n); rp.append(r_n); ip.append(i_n)
        y_s, k_n, v_n, r_n, i_n = _block(y_s, c_sample, pos_s, cache_k[l], cache_v[l],
                                         state_ssm_re[l], state_ssm_im[l], p)
        ksm.append(k_n); vsm.append(v_n); rsm.append(r_n); ism.append(i_n)
    return (y_p, y_s, jnp.stack(kp), jnp.stack(vp), jnp.stack(rp), jnp.stack(ip),
            jnp.stack(ksm), jnp.stack(vsm), jnp.stack(rsm), jnp.stack(ism))
```

```python
import functools
import math

import jax
import jax.numpy as jnp
from jax import lax
from jax.experimental import pallas as pl
from jax.experimental.pallas import tpu as pltpu

F32 = jnp.float32
BF16 = jnp.bfloat16
I32 = jnp.int32

D_MODEL = 2048
SEQ = 4096
PAST_LEN = 8192
D_ATTN = 1024
HEAD_DIM = 128
N_HEADS = 8
ROT_DIM = 32
ROPE_THETA = 500000.0
DILATIONS = (1, 4, 16)
SPAN = 128
WBUF = 2048
D_SSM = 1024
SSM_CH = 16
SSM_GROUPS = 64
SSM_STATE = 64
N_STATE = SSM_GROUPS * SSM_STATE
N_EXPERTS = 256
TOP_K = 8
N_EXPERT_GROUPS = 8
TOPK_GROUPS = 4
D_EXPERT = 512
ROUTE_SCALE = 2.5
EPS = 1e-6
D_IN_PROJ = 4096

MOE_BLK = 128
MOE_SHIFT = MOE_BLK.bit_length() - 1
LANES = 128
SUBLANES = 8
MIB = 1024 * 1024


def _cparams(sem, vmem_mib):
    return pltpu.CompilerParams(dimension_semantics=sem, vmem_limit_bytes=vmem_mib * MIB)


def _rms(x, g):
    return x * lax.rsqrt(jnp.mean(x * x, axis=-1, keepdims=True) + EPS) * g


def _dot(a, b):
    return jnp.dot(a, b, preferred_element_type=F32)


def _dot_nt(a, b):
    return lax.dot_general(a, b, (((1,), (1,)), ((), ())), preferred_element_type=F32)


def _ada_kernel(c_ref, w_ref, b_ref, o_ref):
    a = jax.nn.silu(c_ref[...]).astype(BF16)
    o_ref[...] = _dot(a, w_ref[...].astype(BF16)) + b_ref[...]


def _ada_mod(c_all, w_ada, b_ada):
    rows = c_all.shape[0]
    n_out = w_ada.shape[1]
    tn = 1024
    return pl.pallas_call(
        _ada_kernel,
        grid=(n_out // tn,),
        in_specs=[pl.BlockSpec((rows, D_MODEL), lambda j: (0, 0)),
                  pl.BlockSpec((D_MODEL, tn), lambda j: (0, j)),
                  pl.BlockSpec((1, tn), lambda j: (0, j))],
        out_specs=pl.BlockSpec((rows, tn), lambda j: (0, j)),
        out_shape=jax.ShapeDtypeStruct((rows, n_out), F32),
        compiler_params=_cparams(("arbitrary",), 40),
        name="ada_mod",
    )(c_all, w_ada, b_ada.reshape(1, n_out))


def _rope_kernel(c_ref, s1_ref, s2_ref, *, pos0, stride):
    n = c_ref.shape[0]
    lane = lax.broadcasted_iota(I32, (n, LANES), 1)
    row = lax.broadcasted_iota(I32, (n, LANES), 0) + pl.program_id(0) * n
    pos = (row * stride + pos0).astype(F32)
    half = ROT_DIM // 2
    fi = (lane & (half - 1)).astype(F32)
    inv = jnp.exp(-math.log(ROPE_THETA) * fi / half)
    ang = pos * inv
    cos = jnp.cos(ang)
    sin = jnp.sin(ang)
    c_ref[...] = jnp.where(lane < ROT_DIM, cos, 1.0)
    s1_ref[...] = jnp.where(lane < half, -sin, 0.0)
    s2_ref[...] = jnp.where((lane >= half) & (lane < ROT_DIM), sin, 0.0)


def _rope_tables(n_rows, pos0, stride):
    tr = min(n_rows, 512)
    spec = pl.BlockSpec((tr, LANES), lambda i: (i, 0))
    shp = jax.ShapeDtypeStruct((n_rows, LANES), F32)
    return pl.pallas_call(
        functools.partial(_rope_kernel, pos0=pos0, stride=stride),
        grid=(n_rows // tr,),
        in_specs=[],
        out_specs=[spec, spec, spec],
        out_shape=[shp, shp, shp],
        compiler_params=_cparams(("arbitrary",), 32),
        name="rope_tables",
    )()


def _inproj_kernel(x_ref, sc_ref, sh_ref, ng_ref, w_ref, qkg_ref, rc_ref, rs1_ref, rs2_ref,
                   o_ref, h_scr):
    n = pl.program_id(1)

    @pl.when(n == 0)
    def _():
        h = _rms(x_ref[...], ng_ref[...]) * (1.0 + sc_ref[...]) + sh_ref[...]
        h_scr[...] = h.astype(BF16)

    o_ref[...] = _dot(h_scr[...], w_ref[...])

    @pl.when(n < 2)
    def _():
        g = qkg_ref[n]
        c = rc_ref[...]
        s1 = rs1_ref[...]
        s2 = rs2_ref[...]
        for hh in range(N_HEADS):
            sl = slice(hh * HEAD_DIM, (hh + 1) * HEAD_DIM)
            y = _rms(o_ref[:, sl], g)
            o_ref[:, sl] = (y * c + pltpu.roll(y, HEAD_DIM - ROT_DIM // 2, 1) * s1
                            + pltpu.roll(y, ROT_DIM // 2, 1) * s2)


def _in_proj(x, mod, ng, w_bf, qkg, rope, *, tm, rows_per_mod, rope_blocks):
    t = x.shape[0]
    tn = 1024
    rm = mod.shape[1]
    mt = t // tm

    def mod_map(j):
        return lambda m, n: ((m * tm) // rows_per_mod, 0, j)

    rope_spec = pl.BlockSpec((tm, LANES), lambda m, n: (m % rope_blocks, 0))
    return pl.pallas_call(
        _inproj_kernel,
        grid=(mt, D_IN_PROJ // tn),
        in_specs=[pl.BlockSpec((tm, D_MODEL), lambda m, n: (m, 0)),
                  pl.BlockSpec((None, rm, D_MODEL), mod_map(1)),
                  pl.BlockSpec((None, rm, D_MODEL), mod_map(0)),
                  pl.BlockSpec((1, D_MODEL), lambda m, n: (0, 0)),
                  pl.BlockSpec((D_MODEL, tn), lambda m, n: (0, n)),
                  pl.BlockSpec((2, 1, HEAD_DIM), lambda m, n: (0, 0, 0)),
                  rope_spec, rope_spec, rope_spec],
        out_specs=pl.BlockSpec((tm, tn), lambda m, n: (m, n)),
        out_shape=jax.ShapeDtypeStruct((t, D_IN_PROJ), F32),
        scratch_shapes=[pltpu.VMEM((tm, D_MODEL), BF16)],
        compiler_params=_cparams(("arbitrary", "arbitrary"), 48),
        name="in_proj",
    )(x, mod, mod, ng, w_bf, qkg, *rope)


def _attn_prompt_kernel(q_ref, k_ref, v_ref, o_ref, o0, o1, o2, l0, l1, l2):
    scale = 1.0 / math.sqrt(HEAD_DIM)
    qi = lax.broadcasted_iota(I32, (SPAN, SPAN), 0)
    ki = lax.broadcasted_iota(I32, (SPAN, SPAN), 1)
    own_mask = ki <= qi
    prev_mask = ki >= qi
    neg = -jnp.inf
    oaccs = (o0, o1, o2)
    laccs = (l0, l1, l2)

    for p, d in enumerate(DILATIONS):
        nbk = SEQ // (SPAN * d)
        shift = nbk.bit_length() - 1
        oacc = oaccs[p]
        lacc = laccs[p]

        def rows(start, d=d):
            return pl.ds(start, SPAN) if d == 1 else pl.ds(start, SPAN, stride=d)

        def body(idx, carry, d=d, nbk=nbk, shift=shift, oacc=oacc, lacc=lacc, rows=rows):
            r = idx >> shift
            n = idx & (nbk - 1)
            start = r + n * (SPAN * d)
            pstart = jnp.maximum(start - SPAN * d, r)
            q = q_ref[rows(start), :].astype(BF16)
            ko = k_ref[rows(start), :].astype(BF16)
            vo = v_ref[rows(start), :].astype(BF16)
            kp = k_ref[rows(pstart), :].astype(BF16)
            vp = v_ref[rows(pstart), :].astype(BF16)
            s_o = jnp.where(own_mask, _dot_nt(q, ko) * scale, neg)
            s_p = jnp.where(prev_mask & (n > 0), _dot_nt(q, kp) * scale, neg)
            m = jnp.maximum(jnp.max(s_o, axis=-1, keepdims=True),
                            jnp.max(s_p, axis=-1, keepdims=True))
            p_o = jnp.exp(s_o - m)
            p_p = jnp.exp(s_p - m)
            den = jnp.sum(p_o, axis=-1, keepdims=True) + jnp.sum(p_p, axis=-1, keepdims=True)
            o = (_dot(p_o.astype(BF16), vo) + _dot(p_p.astype(BF16), vp)) / den
            lse = m + jnp.log(den)
            oacc[rows(start), :] = o
            lacc[rows(start), :] = jnp.broadcast_to(lse, (SPAN, HEAD_DIM))
            return carry

        lax.fori_loop(0, SEQ // SPAN, body, 0)

    ch = 512
    for c0 in range(0, SEQ, ch):
        sl = slice(c0, c0 + ch)
        la, lb, lc = l0[sl, :], l1[sl, :], l2[sl, :]
        mx = jnp.maximum(jnp.maximum(la, lb), lc)
        wa, wb, wc = jnp.exp(la - mx), jnp.exp(lb - mx), jnp.exp(lc - mx)
        o_ref[sl, :] = (wa * o0[sl, :] + wb * o1[sl, :] + wc * o2[sl, :]) / (wa + wb + wc)


def _attn_prompt(proj, batch):
    blk = (SEQ, HEAD_DIM)
    scr = pltpu.VMEM((SEQ, HEAD_DIM), F32)
    return pl.pallas_call(
        _attn_prompt_kernel,
        grid=(batch, N_HEADS),
        in_specs=[pl.BlockSpec(blk, lambda b, h: (b, h)),
                  pl.BlockSpec(blk, lambda b, h: (b, N_HEADS + h)),
                  pl.BlockSpec(blk, lambda b, h: (b, 2 * N_HEADS + h))],
        out_specs=pl.BlockSpec(blk, lambda b, h: (b, h)),
        out_shape=jax.ShapeDtypeStruct((batch * SEQ, D_ATTN), F32),
        scratch_shapes=[scr] * 6,
        compiler_params=_cparams(("arbitrary", "arbitrary"), 48),
        name="attn_prompt",
    )(proj, proj, proj)


def _attn_sample_kernel(q_ref, kn_ref, vn_ref, k1, k4, k16, v1, v4, v16, o_ref):
    scale = 1.0 / math.sqrt(HEAD_DIM)
    q = q_ref[...]
    kn = kn_ref[...]
    vn = vn_ref[...]
    s_new = jnp.sum(q * kn, axis=-1, keepdims=True) * scale
    outs = []
    lses = []
    for kc, vc in ((k1, v1), (k4, v4), (k16, v16)):
        kk = kc[...]
        vv = vc[...]
        s = jnp.sum(kk * q[None], axis=-1, keepdims=True) * scale
        m = jnp.maximum(jnp.max(s, axis=0), s_new)
        p = jnp.exp(s - m[None])
        p_new = jnp.exp(s_new - m)
        den = jnp.sum(p, axis=0) + p_new
        o = (jnp.sum(p * vv, axis=0) + p_new * vn) / den
        outs.append(o)
        lses.append(m + jnp.log(den))
    mx = jnp.maximum(jnp.maximum(lses[0], lses[1]), lses[2])
    ws = [jnp.exp(l - mx) for l in lses]
    o_ref[...] = (ws[0] * outs[0] + ws[1] * outs[1] + ws[2] * outs[2]) / (ws[0] + ws[1] + ws[2])


def _attn_sample(q, kn, vn, cache_k, cache_v):
    b = q.shape[0]
    row = pl.BlockSpec((None, N_HEADS, HEAD_DIM), lambda i: (i, 0, 0))
    views, specs = [], []
    for c in (cache_k, cache_v):
        for d in DILATIONS:
            views.append(c.reshape(b, WBUF // d, d, N_HEADS, HEAD_DIM))
            last = WBUF // d // SPAN - 1
            specs.append(pl.BlockSpec((None, SPAN, None, N_HEADS, HEAD_DIM),
                                      lambda i, last=last: (i, last, 0, 0, 0)))
    return pl.pallas_call(
        _attn_sample_kernel,
        grid=(b,),
        in_specs=[row, row, row] + specs,
        out_specs=row,
        out_shape=jax.ShapeDtypeStruct((b, N_HEADS, HEAD_DIM), F32),
        compiler_params=_cparams(("arbitrary",), 32),
        name="attn_sample",
    )(q, kn, vn, *views)


def _cache_kernel(ck_ref, cv_ref, kn_ref, vn_ref, ok_ref, ov_ref, sem):
    copies = [
        pltpu.make_async_copy(ck_ref.at[:, pl.ds(1, WBUF - 1)], ok_ref.at[:, pl.ds(0, WBUF - 1)], sem.at[0]),
        pltpu.make_async_copy(cv_ref.at[:, pl.ds(1, WBUF - 1)], ov_ref.at[:, pl.ds(0, WBUF - 1)], sem.at[1]),
        pltpu.make_async_copy(kn_ref, ok_ref.at[:, pl.ds(WBUF - 1, 1)], sem.at[2]),
        pltpu.make_async_copy(vn_ref, ov_ref.at[:, pl.ds(WBUF - 1, 1)], sem.at[3]),
    ]
    for c in copies:
        c.start()
    for c in copies:
        c.wait()


def _cache_update(cache_k, cache_v, kn, vn):
    anyspec = pl.BlockSpec(memory_space=pl.ANY)
    shp = jax.ShapeDtypeStruct(cache_k.shape, cache_k.dtype)
    return pl.pallas_call(
        _cache_kernel,
        in_specs=[anyspec] * 4,
        out_specs=[anyspec, anyspec],
        out_shape=[shp, shp],
        scratch_shapes=[pltpu.SemaphoreType.DMA((4,))],
        name="cache_update",
    )(cache_k, cache_v, kn, vn)


def _ssm_prep_kernel(ar_ref, ai_ref, ldt_ref, br_ref, bi_ref, cr_ref, ci_ref,
                     ab_ref, wb_ref, wc_ref):
    ar = ar_ref[...]
    ai = ai_ref[...]
    dt = jnp.exp(ldt_ref[...])
    mag = jnp.exp(dt * ar)
    abar_re = mag * jnp.cos(dt * ai)
    abar_im = mag * jnp.sin(dt * ai)
    nr = abar_re - 1.0
    den = ar * ar + ai * ai
    coef_re = (nr * ar + abar_im * ai) / den
    coef_im = (abar_im * ar - nr * ai) / den
    ab_ref[0] = abar_re
    ab_ref[1] = abar_im
    wb_ref[...] = jnp.zeros(wb_ref.shape, wb_ref.dtype)
    wc_ref[...] = jnp.zeros(wc_ref.shape, wc_ref.dtype)
    lane = lax.broadcasted_iota(I32, (SSM_CH, LANES), 1)
    gpt = 16
    for g in range(SSM_GROUPS):
        kt, gl = divmod(g, gpt)
        keep = (lane < SSM_STATE) if g % 2 == 0 else (lane >= SSM_STATE)
        cre = coef_re[g:g + 1, :]
        cim = coef_im[g:g + 1, :]
        bre = br_ref[g]
        bim = bi_ref[g]
        bb_re = jnp.where(keep, cre * bre - cim * bim, 0.0).astype(BF16)
        bb_im = jnp.where(keep, cre * bim + cim * bre, 0.0).astype(BF16)
        rs = slice(gl * SSM_CH, (gl + 1) * SSM_CH)
        lt = (gl // 2) * LANES
        wb_ref[kt, rs, lt:lt + LANES] = bb_re
        wb_ref[kt, rs, 1024 + lt:1024 + lt + LANES] = bb_im
        c_re = jnp.where(keep, cr_ref[g], 0.0).astype(BF16)
        c_im = jnp.where(keep, -ci_ref[g], 0.0).astype(BF16)
        wc_ref[kt, rs, lt:lt + LANES] = c_re
        wc_ref[kt, rs, 1024 + lt:1024 + lt + LANES] = c_im


def _ssm_prep(a_re, a_im, log_dt, b_re, b_im, c_re, c_im):
    dup = lambda x: jnp.concatenate([x, x], axis=-1)
    br_t = dup(jnp.swapaxes(b_re, 1, 2))
    bi_t = dup(jnp.swapaxes(b_im, 1, 2))
    shapes = [jax.ShapeDtypeStruct((2, SSM_GROUPS, LANES), F32),
              jax.ShapeDtypeStruct((4, 256, 2048), BF16),
              jax.ShapeDtypeStruct((4, 256, 2048), BF16)]
    ab, wb, wct = pl.pallas_call(
        _ssm_prep_kernel,
        out_shape=shapes,
        compiler_params=pltpu.CompilerParams(vmem_limit_bytes=40 * MIB),
        name="ssm_prep",
    )(dup(a_re), dup(a_im), log_dt.reshape(SSM_GROUPS, 1), br_t, bi_t, dup(c_re), dup(c_im))
    abar = ab[:, :, :SSM_STATE].reshape(2, N_STATE)
    return abar, wb, wct


def _ssm_bu(ub, wb_ref):
    return [_dot(ub[:, kt * 256:(kt + 1) * 256], wb_ref[kt]) for kt in range(4)]


def _ssm_tail(h_re_tiles, h_im_tiles, u, wc_ref, d_ref, wg_ref, bg_ref, og_ref):
    ys = []
    for kt in range(4):
        wct = wc_ref[kt]
        ys.append(_dot_nt(h_re_tiles[kt].astype(BF16), wct[:, :1024])
                  + _dot_nt(h_im_tiles[kt].astype(BF16), wct[:, 1024:]))
    y = jnp.concatenate(ys, axis=-1) + d_ref[...] * u
    g = jax.nn.gelu(y)
    z = _dot(g.astype(BF16), wg_ref[...]) + bg_ref[...]
    out = g * jax.nn.sigmoid(z)
    return _rms(out, og_ref[...])


SCAN_LANES = 256


def _ssm_prompt_kernel(u_ref, wb_ref, wc_ref, ab_ref, d_ref, wg_ref, bg_ref, og_ref,
                       o_ref, hre_ref, him_ref, bre, bim, cre, cim):
    c = pl.program_id(1)
    tc = u_ref.shape[0]

    @pl.when(c == 0)
    def _():
        cre[...] = jnp.zeros(cre.shape, F32)
        cim[...] = jnp.zeros(cim.shape, F32)

    u = u_ref[...]
    bu = _ssm_bu(u.astype(BF16), wb_ref)
    for kt in range(4):
        bre[:, kt * 1024:(kt + 1) * 1024] = bu[kt][:, :1024]
        bim[:, kt * 1024:(kt + 1) * 1024] = bu[kt][:, 1024:]

    row = lax.broadcasted_iota(I32, (SUBLANES, SCAN_LANES), 0)

    def cmul(xr, xi, yr, yi):
        return xr * yr - xi * yi, xr * yi + xi * yr

    for lc in range(N_STATE // SCAN_LANES):
        sl = slice(lc * SCAN_LANES, (lc + 1) * SCAN_LANES)
        a1r = jnp.broadcast_to(ab_ref[0:1, sl], (SUBLANES, SCAN_LANES))
        a1i = jnp.broadcast_to(ab_ref[1:2, sl], (SUBLANES, SCAN_LANES))
        a2r, a2i = cmul(a1r, a1i, a1r, a1i)
        a4r, a4i = cmul(a2r, a2i, a2r, a2i)
        a8r, a8i = cmul(a4r, a4i, a4r, a4i)
        pwr = jnp.ones((SUBLANES, SCAN_LANES), F32)
        pwi = jnp.zeros((SUBLANES, SCAN_LANES), F32)
        for bit, (fr, fi) in ((1, (a1r, a1i)), (2, (a2r, a2i)), (4, (a4r, a4i)), (8, (a8r, a8i))):
            has = ((row + 1) & bit) != 0
            nr, ni = cmul(pwr, pwi, fr, fi)
            pwr = jnp.where(has, nr, pwr)
            pwi = jnp.where(has, ni, pwi)
        steps = ((1, a1r, a1i), (2, a2r, a2i), (4, a4r, a4i))

        def body(m, carry, sl=sl, steps=steps, pwr=pwr, pwi=pwi):
            cr, ci = carry
            off = pl.multiple_of(m * SUBLANES, SUBLANES)
            xr = bre[pl.ds(off, SUBLANES), sl]
            xi = bim[pl.ds(off, SUBLANES), sl]
            for s, fr, fi in steps:
                sr = jnp.where(row >= s, pltpu.roll(xr, s, 0), 0.0)
                si = jnp.where(row >= s, pltpu.roll(xi, s, 0), 0.0)
                tr, ti = cmul(fr, fi, sr, si)
                xr = xr + tr
                xi = xi + ti
            tr, ti = cmul(pwr, pwi, cr, ci)
            xr = xr + tr
            xi = xi + ti
            bre[pl.ds(off, SUBLANES), sl] = xr
            bim[pl.ds(off, SUBLANES), sl] = xi
            cr = jnp.broadcast_to(xr[SUBLANES - 1:SUBLANES, :], (SUBLANES, SCAN_LANES))
            ci = jnp.broadcast_to(xi[SUBLANES - 1:SUBLANES, :], (SUBLANES, SCAN_LANES))
            return cr, ci

        cr, ci = lax.fori_loop(0, tc // SUBLANES, body, (cre[:, sl], cim[:, sl]), unroll=2)
        cre[:, sl] = cr
        cim[:, sl] = ci

    h_re = [bre[:, kt * 1024:(kt + 1) * 1024] for kt in range(4)]
    h_im = [bim[:, kt * 1024:(kt + 1) * 1024] for kt in range(4)]
    o_ref[...] = _ssm_tail(h_re, h_im, u, wc_ref, d_ref, wg_ref, bg_ref, og_ref).astype(BF16)

    @pl.when(c == pl.num_programs(1) - 1)
    def _():
        hre_ref[...] = cre[0:1, :]
        him_ref[...] = cim[0:1, :]


def _ssm_prompt(proj, batch, abar, wb, wct, ssm_d, wglu_bf, bglu, og):
    tc = 256
    nchunk = SEQ // tc
    const2 = lambda shape: pl.BlockSpec(shape, lambda b, c: (0,) * len(shape))
    st_spec = pl.BlockSpec((None, 1, N_STATE), lambda b, c: (b, 0, 0))
    st_shape = jax.ShapeDtypeStruct((batch, 1, N_STATE), F32)
    return pl.pallas_call(
        _ssm_prompt_kernel,
        grid=(batch, nchunk),
        in_specs=[pl.BlockSpec((tc, D_SSM), lambda b, c: (b * nchunk + c, 3)),
                  const2((4, 256, 2048)), const2((4, 256, 2048)), const2((2, N_STATE)),
                  const2((1, D_SSM)), const2((D_SSM, D_SSM)), const2((1, D_SSM)), const2((1, D_SSM))],
        out_specs=[pl.BlockSpec((tc, D_SSM), lambda b, c: (b * nchunk + c, 0)), st_spec, st_spec],
        out_shape=[jax.ShapeDtypeStruct((batch * SEQ, D_SSM), BF16), st_shape, st_shape],
        scratch_shapes=[pltpu.VMEM((tc, N_STATE), F32), pltpu.VMEM((tc, N_STATE), F32),
                        pltpu.VMEM((SUBLANES, N_STATE), F32), pltpu.VMEM((SUBLANES, N_STATE), F32)],
        compiler_params=_cparams(("arbitrary", "arbitrary"), 48),
        name="ssm_prompt",
    )(proj, wb, wct, abar, ssm_d, wglu_bf, bglu, og)


def _ssm_sample_kernel(u_ref, h0r_ref, h0i_ref, wb_ref, wc_ref, ab_ref, d_ref, wg_ref, bg_ref, og_ref,
                       o_ref, hre_ref, him_ref):
    u = u_ref[...]
    bu = _ssm_bu(u.astype(BF16), wb_ref)
    h_re, h_im = [], []
    for kt in range(4):
        sl = slice(kt * 1024, (kt + 1) * 1024)
        ar = ab_ref[0:1, sl]
        ai = ab_ref[1:2, sl]
        h0r = h0r_ref[:, sl]
        h0i = h0i_ref[:, sl]
        hr = bu[kt][:, :1024] + (ar * h0r - ai * h0i)
        hi = bu[kt][:, 1024:] + (ar * h0i + ai * h0r)
        hre_ref[:, sl] = hr
        him_ref[:, sl] = hi
        h_re.append(hr)
        h_im.append(hi)
    o_ref[...] = _ssm_tail(h_re, h_im, u, wc_ref, d_ref, wg_ref, bg_ref, og_ref).astype(BF16)


def _ssm_sample(u, h0_re, h0_im, abar, wb, wct, ssm_d, wglu_bf, bglu, og):
    b = u.shape[0]
    st_shape = jax.ShapeDtypeStruct((b, N_STATE), F32)
    return pl.pallas_call(
        _ssm_sample_kernel,
        out_shape=[jax.ShapeDtypeStruct((b, D_SSM), BF16), st_shape, st_shape],
        compiler_params=pltpu.CompilerParams(vmem_limit_bytes=40 * MIB),
        name="ssm_sample",
    )(u, h0_re, h0_im, wb, wct, abar, ssm_d, wglu_bf, bglu, og)


def _outproj_kernel(attn_ref, ssm_ref, ag_ref, w_ref, x_ref, g1_ref, o_ref, lhs):
    n = pl.program_id(1)

    @pl.when(n == 0)
    def _():
        lhs[:, :D_ATTN] = _rms(attn_ref[...], ag_ref[...]).astype(BF16)
        lhs[:, D_ATTN:] = ssm_ref[...]

    o_ref[...] = x_ref[...] + g1_ref[...] * _dot(lhs[...], w_ref[...])


def _out_proj(attn, ssm_n, ag, w_bf, x, mod, *, tm, rows_per_mod):
    t = x.shape[0]
    tn = 1024
    rm = mod.shape[1]
    return pl.pallas_call(
        _outproj_kernel,
        grid=(t // tm, D_MODEL // tn),
        in_specs=[pl.BlockSpec((tm, D_ATTN), lambda m, n: (m, 0)),
                  pl.BlockSpec((tm, D_SSM), lambda m, n: (m, 0)),
                  pl.BlockSpec((1, D_ATTN), lambda m, n: (0, 0)),
                  pl.BlockSpec((D_MODEL, tn), lambda m, n: (0, n)),
                  pl.BlockSpec((tm, tn), lambda m, n: (m, n)),
                  pl.BlockSpec((None, rm, tn), lambda m, n: ((m * tm) // rows_per_mod, 0, 2 * (D_MODEL // tn) + n))],
        out_specs=pl.BlockSpec((tm, tn), lambda m, n: (m, n)),
        out_shape=jax.ShapeDtypeStruct((t, D_MODEL), F32),
        scratch_shapes=[pltpu.VMEM((tm, D_MODEL), BF16)],
        compiler_params=_cparams(("arbitrary", "arbitrary"), 48),
        name="out_proj",
    )(attn, ssm_n, ag, w_bf, x, mod)


def _route_kernel(x_ref, sc_ref, sh_ref, ng_ref, wr_ref, br_ref, h2_ref, te_ref, gt_ref):
    tm = x_ref.shape[0]
    h2 = _rms(x_ref[...], ng_ref[...]) * (1.0 + sc_ref[...]) + sh_ref[...]
    h2_ref[...] = h2
    scores = jax.nn.sigmoid(_dot_nt(wr_ref[...], h2.astype(BF16)))
    biased = scores + br_ref[...]
    neg = -jnp.inf
    per_g = N_EXPERTS // N_EXPERT_GROUPS
    gi_iota = lax.broadcasted_iota(I32, (per_g, tm), 0)
    gscore = []
    for g in range(N_EXPERT_GROUPS):
        xg = biased[g * per_g:(g + 1) * per_g, :]
        m1 = jnp.max(xg, axis=0, keepdims=True)
        i1 = jnp.min(jnp.where(xg == m1, gi_iota, per_g), axis=0, keepdims=True)
        m2 = jnp.max(jnp.where(gi_iota == i1, neg, xg), axis=0, keepdims=True)
        gscore.append(m1 + m2)
    parts = []
    for i in range(N_EXPERT_GROUPS):
        rank = jnp.zeros((1, tm), I32)
        for j in range(N_EXPERT_GROUPS):
            if j == i:
                continue
            ahead = (gscore[j] >= gscore[i]) if j < i else (gscore[j] > gscore[i])
            rank = rank + ahead.astype(I32)
        parts.append(jnp.where(rank < TOPK_GROUPS, biased[i * per_g:(i + 1) * per_g, :], neg))
    cur = jnp.concatenate(parts, axis=0)
    e_iota = lax.broadcasted_iota(I32, (N_EXPERTS, tm), 0)
    idxs, gates = [], []
    for _ in range(TOP_K):
        m = jnp.max(cur, axis=0, keepdims=True)
        idx = jnp.min(jnp.where(cur == m, e_iota, N_EXPERTS), axis=0, keepdims=True)
        hit = e_iota == idx
        gates.append(jnp.sum(jnp.where(hit, scores, 0.0), axis=0, keepdims=True))
        idxs.append(idx)
        cur = jnp.where(hit, neg, cur)
    tot = gates[0]
    for g in gates[1:]:
        tot = tot + g
    for k in range(TOP_K):
        te_ref[k:k + 1, :] = idxs[k]
        gt_ref[k:k + 1, :] = gates[k] / tot * ROUTE_SCALE


def _route(x1, mod, ng, wr_t_bf, b_router, *, tm, rows_per_mod):
    t = x1.shape[0]
    rm = mod.shape[1]

    def mod_map(j):
        return lambda m: ((m * tm) // rows_per_mod, 0, j)

    return pl.pallas_call(
        _route_kernel,
        grid=(t // tm,),
        in_specs=[pl.BlockSpec((tm, D_MODEL), lambda m: (m, 0)),
                  pl.BlockSpec((None, rm, D_MODEL), mod_map(4)),
                  pl.BlockSpec((None, rm, D_MODEL), mod_map(3)),
                  pl.BlockSpec((1, D_MODEL), lambda m: (0, 0)),
                  pl.BlockSpec((N_EXPERTS, D_MODEL), lambda m: (0, 0)),
                  pl.BlockSpec((N_EXPERTS, 1), lambda m: (0, 0))],
        out_specs=[pl.BlockSpec((tm, D_MODEL), lambda m: (m, 0)),
                   pl.BlockSpec((TOP_K, tm), lambda m: (0, m)),
                   pl.BlockSpec((TOP_K, tm), lambda m: (0, m))],
        out_shape=[jax.ShapeDtypeStruct((t, D_MODEL), F32),
                   jax.ShapeDtypeStruct((TOP_K, t), I32),
                   jax.ShapeDtypeStruct((TOP_K, t), F32)],
        compiler_params=_cparams(("arbitrary",), 40),
        name="route",
    )(x1, mod, mod, ng, wr_t_bf, b_router.reshape(N_EXPERTS, 1))


DISP_TM = 256
N_EID = 896


def _rank_kernel(te_ref, rank_ref, cnt_ref, run):
    i = pl.program_id(0)
    tm = te_ref.shape[1]

    @pl.when(i == 0)
    def _():
        run[...] = jnp.zeros(run.shape, F32)

    te = te_ref[...]
    e_iota = lax.broadcasted_iota(I32, (N_EXPERTS, tm), 0)
    a = jnp.zeros((N_EXPERTS, tm), F32)
    for k in range(TOP_K):
        a = a + (e_iota == te[k:k + 1, :]).astype(F32)
    upper = (lax.broadcasted_iota(I32, (tm, tm), 0) < lax.broadcasted_iota(I32, (tm, tm), 1)).astype(BF16)
    tot = _dot(a.astype(BF16), upper) + run[:, 0:1]
    for k in range(TOP_K):
        rk = jnp.sum(jnp.where(e_iota == te[k:k + 1, :], tot, 0.0), axis=0, keepdims=True)
        rank_ref[k:k + 1, :] = rk.astype(I32)
    run[...] = run[...] + jnp.sum(a, axis=1, keepdims=True)

    @pl.when(i == pl.num_programs(0) - 1)
    def _():
        cnt_ref[...] = run[...].astype(I32)


def _dest_kernel(te_ref, rank_ref, cnt_ref, dest_ref, eid_ref, ps_ref, nblk_ref, ps_scr):
    i = pl.program_id(0)
    tm = te_ref.shape[1]

    @pl.when(i == 0)
    def _():
        nb_e = ((cnt_ref[...] + (MOE_BLK - 1)) >> MOE_SHIFT).astype(F32)
        lower = (lax.broadcasted_iota(I32, (N_EXPERTS, N_EXPERTS), 1)
                 < lax.broadcasted_iota(I32, (N_EXPERTS, N_EXPERTS), 0)).astype(BF16)
        ps_b = _dot(lower, nb_e.astype(BF16))
        ps_scr[...] = ps_b * MOE_BLK
        ps_ref[...] = (ps_b * MOE_BLK).astype(I32)
        pe_b = ps_b + nb_e
        blk_i = lax.broadcasted_iota(I32, (N_EXPERTS, N_EID), 1).astype(F32)
        eid = jnp.sum((pe_b[:, 0:1] <= blk_i).astype(F32), axis=0, keepdims=True)
        eid_ref[...] = jnp.minimum(eid, N_EXPERTS - 1).astype(I32)
        nblk_ref[...] = jnp.max(pe_b, axis=0, keepdims=True).astype(I32)

    te = te_ref[...]
    rk = rank_ref[...]
    ps = ps_scr[:, 0:1]
    e_iota = lax.broadcasted_iota(I32, (N_EXPERTS, tm), 0)
    for k in range(TOP_K):
        tek = te[k:k + 1, :]
        base = jnp.sum(jnp.where(e_iota == tek, ps, 0.0), axis=0, keepdims=True)
        dest_ref[k:k + 1, :] = jnp.where(tek >= 0, base.astype(I32) + rk[k:k + 1, :], -1)


def _dispatch_tables(te):
    tpad = te.shape[1]
    tm = DISP_TM
    tile = pl.BlockSpec((TOP_K, tm), lambda i: (0, i))
    full = lambda shape: pl.BlockSpec(shape, lambda i: (0,) * len(shape))
    rank, cnt = pl.pallas_call(
        _rank_kernel,
        grid=(tpad // tm,),
        in_specs=[tile],
        out_specs=[tile, full((N_EXPERTS, LANES))],
        out_shape=[jax.ShapeDtypeStruct((TOP_K, tpad), I32),
                   jax.ShapeDtypeStruct((N_EXPERTS, LANES), I32)],
        scratch_shapes=[pltpu.VMEM((N_EXPERTS, LANES), F32)],
        compiler_params=_cparams(("arbitrary",), 32),
        name="moe_rank",
    )(te)
    dest, eid, ps, nblk = pl.pallas_call(
        _dest_kernel,
        grid=(tpad // tm,),
        in_specs=[tile, tile, full((N_EXPERTS, LANES))],
        out_specs=[tile, full((1, N_EID)), full((N_EXPERTS, LANES)), full((1, LANES))],
        out_shape=[jax.ShapeDtypeStruct((TOP_K, tpad), I32),
                   jax.ShapeDtypeStruct((1, N_EID), I32),
                   jax.ShapeDtypeStruct((N_EXPERTS, LANES), I32),
                   jax.ShapeDtypeStruct((1, LANES), I32)],
        scratch_shapes=[pltpu.VMEM((N_EXPERTS, LANES), F32)],
        compiler_params=_cparams(("arbitrary",), 32),
        name="moe_dest",
    )(te, rank, cnt)
    return dest, eid.reshape(N_EID), nblk[0, :1], cnt[:, 0], ps[:, 0]


PAD_BITS = (64, 32, 16, 8)


def _scatter_kernel(cnt_ref, ps_ref, nblk_ref, dest_ref, hp_ref, hs_ref, xs_ref, zeros, sem, zsem, *,
                    n_prompt_tiles):
    i = pl.program_id(0)
    tm = dest_ref.shape[1]

    def row_copy(src_ref, r, d):
        return pltpu.make_async_copy(src_ref.at[pl.ds(r, 1)], xs_ref.at[pl.ds(d, 1)], sem)

    def scatter_rows(src_ref, n_rows):
        def issue(j, c):
            r = j >> 3
            d = dest_ref[j & 7, r]

            @pl.when(d >= 0)
            def _():
                row_copy(src_ref, r, d).start()
            return c

        def drain(j, c):
            r = j >> 3
            d = dest_ref[j & 7, r]

            @pl.when(d >= 0)
            def _():
                row_copy(src_ref, r, d).wait()
            return c

        lax.fori_loop(0, n_rows * TOP_K, issue, 0)
        lax.fori_loop(0, n_rows * TOP_K, drain, 0)

    @pl.when(i < n_prompt_tiles)
    def _():
        scatter_rows(hp_ref, tm)

    @pl.when(i == n_prompt_tiles)
    def _():
        scatter_rows(hs_ref, hs_ref.shape[0])

    @pl.when(i == 0)
    def _():
        zeros[...] = jnp.zeros(zeros.shape, F32)

        def pad_copies(e, do):
            n = cnt_ref[e]
            npad = (((n + (MOE_BLK - 1)) >> MOE_SHIFT) << MOE_SHIFT) - n
            base = ps_ref[e] + n
            head = npad & (SUBLANES - 1)
            for s in range(SUBLANES - 1):
                @pl.when(s < head)
                def _(s=s):
                    do(pltpu.make_async_copy(zeros.at[pl.ds(0, 1)], xs_ref.at[pl.ds(base + s, 1)], zsem))
            off = base + head
            for bit in PAD_BITS:
                take = (npad & bit) != 0

                @pl.when(take)
                def _(off=off, bit=bit):
                    dst = xs_ref.at[pl.ds(pl.multiple_of(off, SUBLANES), bit)]
                    do(pltpu.make_async_copy(zeros.at[pl.ds(0, bit)], dst, zsem))
                off = off + jnp.where(take, bit, 0)

        def issue(e, c):
            pad_copies(e, lambda cp: cp.start())
            return c

        def drain(e, c):
            pad_copies(e, lambda cp: cp.wait())
            return c

        lax.fori_loop(0, N_EXPERTS, issue, 0)
        lax.fori_loop(0, N_EXPERTS, drain, 0)

        def tail_copy(b):
            dst = xs_ref.at[pl.ds(pl.multiple_of(b * MOE_BLK, MOE_BLK), MOE_BLK)]
            return pltpu.make_async_copy(zeros, dst, zsem)

        def tail_issue(b, c):
            @pl.when(b >= nblk_ref[0])
            def _():
                tail_copy(b).start()
            return c

        def tail_drain(b, c):
            @pl.when(b >= nblk_ref[0])
            def _():
                tail_copy(b).wait()
            return c

        n_blocks = xs_ref.shape[0] // MOE_BLK
        lax.fori_loop(0, n_blocks, tail_issue, 0)
        lax.fori_loop(0, n_blocks, tail_drain, 0)


def _scatter_tokens(dest, cnt, ps, nblk, h2_p, h2_s, n_slots):
    tm = DISP_TM
    n_prompt_tiles = h2_p.shape[0] // tm
    grid_spec = pltpu.PrefetchScalarGridSpec(
        num_scalar_prefetch=3,
        grid=(n_prompt_tiles + 1,),
        in_specs=[pl.BlockSpec((TOP_K, tm), lambda i, *_: (0, i), memory_space=pltpu.SMEM),
                  pl.BlockSpec((tm, D_MODEL), lambda i, *_: (jnp.minimum(i, n_prompt_tiles - 1), 0)),
                  pl.BlockSpec(h2_s.shape, lambda i, *_: (0, 0))],
        out_specs=pl.BlockSpec(memory_space=pl.ANY),
        scratch_shapes=[pltpu.VMEM((MOE_BLK, D_MODEL), F32),
                        pltpu.SemaphoreType.DMA(()), pltpu.SemaphoreType.DMA(())],
    )
    return pl.pallas_call(
        functools.partial(_scatter_kernel, n_prompt_tiles=n_prompt_tiles),
        grid_spec=grid_spec,
        out_shape=jax.ShapeDtypeStruct((n_slots, D_MODEL), F32),
        compiler_params=_cparams(("arbitrary",), 32),
        name="moe_scatter",
    )(cnt, ps, nblk, dest, h2_p, h2_s)


def _moe_kernel(eid_ref, nblk_ref, x_ref, wg_ref, wu_ref, wd_ref, y_ref, wgb, wub, wdb):
    i = pl.program_id(0)
    prev = eid_ref[jnp.maximum(i, 1) - 1]
    fresh = (i == 0) | (eid_ref[i] != prev)

    @pl.when(fresh)
    def _():
        wgb[...] = wg_ref[...].astype(BF16)
        wub[...] = wu_ref[...].astype(BF16)
        wdb[...] = wd_ref[...].astype(BF16)

    @pl.when(i < nblk_ref[0])
    def _():
        x = x_ref[...].astype(BF16)
        hid = jax.nn.silu(_dot(x, wgb[...])) * _dot(x, wub[...])
        y_ref[...] = _dot(hid.astype(BF16), wdb[...])

    @pl.when(i >= nblk_ref[0])
    def _():
        y_ref[...] = jnp.zeros(y_ref.shape, F32)


def _moe_experts(eid, nblk, xs, w_gate, w_up, w_down):
    n_slots = xs.shape[0]
    nb = n_slots // MOE_BLK
    row_map = lambda i, eid, nblk: (jnp.minimum(i, nblk[0] - 1), 0)
    w_map = lambda i, eid, nblk: (eid[i], 0, 0)
    grid_spec = pltpu.PrefetchScalarGridSpec(
        num_scalar_prefetch=2,
        grid=(nb,),
        in_specs=[pl.BlockSpec((MOE_BLK, D_MODEL), row_map),
                  pl.BlockSpec((None, D_MODEL, D_EXPERT), w_map),
                  pl.BlockSpec((None, D_MODEL, D_EXPERT), w_map),
                  pl.BlockSpec((None, D_EXPERT, D_MODEL), w_map)],
        out_specs=pl.BlockSpec((MOE_BLK, D_MODEL), lambda i, eid, nblk: (i, 0)),
        scratch_shapes=[pltpu.VMEM((D_MODEL, D_EXPERT), BF16), pltpu.VMEM((D_MODEL, D_EXPERT), BF16),
                        pltpu.VMEM((D_EXPERT, D_MODEL), BF16)],
    )
    return pl.pallas_call(
        _moe_kernel,
        grid_spec=grid_spec,
        out_shape=jax.ShapeDtypeStruct((n_slots, D_MODEL), F32),
        compiler_params=_cparams(("arbitrary",), 48),
        name="moe_experts",
    )(eid, nblk, xs, w_gate, w_up, w_down)


def _combine_kernel(dcur_ref, dnxt_ref, gate_ref, h2_ref, x1_ref, g2_ref, wsg_ref, wsu_ref, wsd_ref,
                    ys_ref, o_ref, ybuf, sem):
    i = pl.program_id(0)
    n = pl.num_programs(0)
    tm = h2_ref.shape[0]
    slot = i % 2

    def row_copy(d, s, k, r):
        return pltpu.make_async_copy(ys_ref.at[pl.ds(d, 1)], ybuf.at[s, k, pl.ds(r, 1)], sem.at[s])

    def gather(dref, s, wait):
        def step(j, c):
            r = j >> 3
            k = j & 7
            d = dref[k, r]

            @pl.when(d >= 0)
            def _():
                cp = row_copy(d, s, k, r)
                if wait:
                    cp.wait()
                else:
                    cp.start()
            return c

        lax.fori_loop(0, tm * TOP_K, step, 0)

    @pl.when(i == 0)
    def _():
        gather(dcur_ref, 0, False)

    @pl.when(i + 1 < n)
    def _():
        gather(dnxt_ref, 1 - slot, False)

    gather(dcur_ref, slot, True)

    gates = gate_ref[...]
    routed = gates[:, 0:1] * ybuf[slot, 0]
    for k in range(1, TOP_K):
        routed = routed + gates[:, k:k + 1] * ybuf[slot, k]
    hb = h2_ref[...].astype(BF16)
    hid = jax.nn.silu(_dot(hb, wsg_ref[...])) * _dot(hb, wsu_ref[...])
    shared = _dot(hid.astype(BF16), wsd_ref[...])
    o_ref[...] = x1_ref[...] + g2_ref[...] * (routed + shared)


def _combine(dest, gate_tk, h2, x1, mod, wsg_bf, wsu_bf, wsd_bf, ys, *, tm, rows_per_mod):
    t = h2.shape[0]
    nt = t // tm
    rm = mod.shape[1]
    const = lambda shape: pl.BlockSpec(shape, lambda i: (0,) * len(shape))
    return pl.pallas_call(
        _combine_kernel,
        grid=(nt,),
        in_specs=[pl.BlockSpec((TOP_K, tm), lambda i: (0, i), memory_space=pltpu.SMEM),
                  pl.BlockSpec((TOP_K, tm), lambda i: (0, jnp.minimum(i + 1, nt - 1)), memory_space=pltpu.SMEM),
                  pl.BlockSpec((tm, TOP_K), lambda i: (i, 0)),
                  pl.BlockSpec((tm, D_MODEL), lambda i: (i, 0)),
                  pl.BlockSpec((tm, D_MODEL), lambda i: (i, 0)),
                  pl.BlockSpec((None, rm, D_MODEL), lambda i: ((i * tm) // rows_per_mod, 0, 5)),
                  const((D_MODEL, D_EXPERT)), const((D_MODEL, D_EXPERT)), const((D_EXPERT, D_MODEL)),
                  pl.BlockSpec(memory_space=pl.ANY)],
        out_specs=pl.BlockSpec((tm, D_MODEL), lambda i: (i, 0)),
        out_shape=jax.ShapeDtypeStruct((t, D_MODEL), F32),
        scratch_shapes=[pltpu.VMEM((2, TOP_K, tm, D_MODEL), F32), pltpu.SemaphoreType.DMA((2,))],
        compiler_params=_cparams(("arbitrary",), 48),
        name="moe_combine",
    )(dest, dest, gate_tk, h2, x1, mod, wsg_bf, wsu_bf, wsd_bf, ys)


def kernel(x_prompt, x_sample, cache_k, cache_v, state_ssm_re, state_ssm_im, c_prompt, c_sample,
           w_ada, b_ada, norm1_g, w_in, q_norm_g, k_norm_g, ssm_A_re, ssm_A_im, ssm_log_dt,
           ssm_B_re, ssm_B_im, ssm_C_re, ssm_C_im, ssm_D, ssm_w_glu, ssm_b_glu, attn_out_g,
           ssm_out_g, w_out, norm2_g, w_router, b_router, w_exp_gate, w_exp_up, w_exp_down,
           w_sh_gate, w_sh_up, w_sh_down):
    nb_p, seq, _ = x_prompt.shape
    nb_s = x_sample.shape[0]
    t_p = nb_p * seq
    l = 0

    n_c = nb_p + nb_s
    c_rows = -(-n_c // SUBLANES) * SUBLANES
    c_all = jnp.concatenate([c_prompt, c_sample, jnp.zeros((c_rows - n_c, D_MODEL), F32)], axis=0)
    mod = _ada_mod(c_all, w_ada[l], b_ada[l])
    mod_p = mod[:nb_p].reshape(nb_p, 1, 6 * D_MODEL)
    mod_s = mod[nb_p:n_c].reshape(1, nb_s, 6 * D_MODEL)

    w_in_bf = w_in[l].astype(BF16)
    w_out_bf = w_out[l].astype(BF16)
    wglu_bf = ssm_w_glu[l].astype(BF16)
    wr_t_bf = w_router[l].T.astype(BF16)
    wsg_bf = w_sh_gate[l].astype(BF16)
    wsu_bf = w_sh_up[l].astype(BF16)
    wsd_bf = w_sh_down[l].astype(BF16)
    ng1 = norm1_g[l].reshape(1, D_MODEL)
    ng2 = norm2_g[l].reshape(1, D_MODEL)
    qkg = jnp.stack([q_norm_g[l], k_norm_g[l]]).reshape(2, 1, HEAD_DIM)
    ag = attn_out_g[l].reshape(1, D_ATTN)
    og = ssm_out_g[l].reshape(1, D_SSM)
    ssm_d = ssm_D[l].reshape(1, D_SSM)
    bglu = ssm_b_glu[l].reshape(1, D_SSM)

    xp = x_prompt.reshape(t_p, D_MODEL)
    xs = x_sample.reshape(nb_s, D_MODEL)

    tm_p = 512
    rope_p = _rope_tables(seq, 0, 1)
    rope_s = _rope_tables(nb_s, PAST_LEN, 0)
    proj_p = _in_proj(xp, mod_p, ng1, w_in_bf, qkg, rope_p, tm=tm_p, rows_per_mod=seq,
                      rope_blocks=seq // tm_p)
    proj_s = _in_proj(xs, mod_s, ng1, w_in_bf, qkg, rope_s, tm=nb_s, rows_per_mod=nb_s, rope_blocks=1)

    attn_p = _attn_prompt(proj_p, nb_p)
    q_s = proj_s[:, :D_ATTN].reshape(nb_s, N_HEADS, HEAD_DIM)
    k_s = proj_s[:, D_ATTN:2 * D_ATTN].reshape(nb_s, N_HEADS, HEAD_DIM)
    v_s = proj_s[:, 2 * D_ATTN:3 * D_ATTN].reshape(nb_s, N_HEADS, HEAD_DIM)
    attn_s = _attn_sample(q_s, k_s, v_s, cache_k[l], cache_v[l]).reshape(nb_s, D_ATTN)
    k_new, v_new = _cache_update(cache_k[l], cache_v[l], k_s[:, None], v_s[:, None])

    abar, wb, wct = _ssm_prep(ssm_A_re[l], ssm_A_im[l], ssm_log_dt[l], ssm_B_re[l], ssm_B_im[l],
                              ssm_C_re[l], ssm_C_im[l])
    ssm_p, hre_p, him_p = _ssm_prompt(proj_p, nb_p, abar, wb, wct, ssm_d, wglu_bf, bglu, og)
    ssm_s, hre_s, him_s = _ssm_sample(proj_s[:, 3 * D_ATTN:], state_ssm_re[l].reshape(nb_s, N_STATE),
                                      state_ssm_im[l].reshape(nb_s, N_STATE), abar, wb, wct,
                                      ssm_d, wglu_bf, bglu, og)

    x1_p = _out_proj(attn_p, ssm_p, ag, w_out_bf, xp, mod_p, tm=tm_p, rows_per_mod=seq)
    x1_s = _out_proj(attn_s, ssm_s, ag, w_out_bf, xs, mod_s, tm=nb_s, rows_per_mod=nb_s)

    h2_p, te_p, gt_p = _route(x1_p, mod_p, ng2, wr_t_bf, b_router[l], tm=256, rows_per_mod=seq)
    h2_s, te_s, gt_s = _route(x1_s, mod_s, ng2, wr_t_bf, b_router[l], tm=nb_s, rows_per_mod=nb_s)
    t_pad = (t_p // DISP_TM + 1) * DISP_TM
    te = jnp.concatenate([te_p, te_s, jnp.full((TOP_K, t_pad - t_p - nb_s), -1, I32)], axis=1)
    dest, eid, nblk, cnt, ps = _dispatch_tables(te)
    n_assign = (t_p + nb_s) * TOP_K
    n_slots = (-(-n_assign // MOE_BLK) + N_EXPERTS) * MOE_BLK
    assert n_slots // MOE_BLK <= N_EID
    xs_sorted = _scatter_tokens(dest, cnt, ps, nblk, h2_p, h2_s, n_slots)
    ys_sorted = _moe_experts(eid, nblk, xs_sorted, w_exp_gate[l], w_exp_up[l], w_exp_down[l])
    y_p = _combine(dest[:, :t_p], gt_p.T, h2_p, x1_p, mod_p, wsg_bf, wsu_bf, wsd_bf, ys_sorted,
                   tm=128, rows_per_mod=seq)
    y_s = _combine(dest[:, t_p:t_p + nb_s], gt_s.T, h2_s, x1_s, mod_s, wsg_bf, wsu_bf, wsd_bf, ys_sorted,
                   tm=nb_s, rows_per_mod=nb_s)

    keep = min(WBUF, seq)
    kv = proj_p.reshape(nb_p, seq, D_IN_PROJ)[:, seq - keep:, D_ATTN:3 * D_ATTN]
    k_p = kv[..., :D_ATTN].reshape(1, nb_p, keep, N_HEADS, HEAD_DIM)
    v_p = kv[..., D_ATTN:].reshape(1, nb_p, keep, N_HEADS, HEAD_DIM)
    st = lambda h, b: h.reshape(1, b, SSM_GROUPS, SSM_STATE)
    return (y_p.reshape(nb_p, seq, D_MODEL), y_s.reshape(nb_s, 1, D_MODEL), k_p, v_p,
            st(hre_p, nb_p), st(him_p, nb_p), k_new[None], v_new[None], st(hre_s, nb_s), st(him_s, nb_s))
```

```python
import functools
import math

import jax
import jax.numpy as jnp
from jax import lax
from jax.experimental import pallas as pl
from jax.experimental.pallas import tpu as pltpu

F32 = jnp.float32
BF16 = jnp.bfloat16
I32 = jnp.int32

D_MODEL = 2048
SEQ = 4096
PAST_LEN = 8192
D_ATTN = 1024
HEAD_DIM = 128
N_HEADS = 8
ROT_DIM = 32
ROPE_THETA = 500000.0
DILATIONS = (1, 4, 16)
SPAN = 128
WBUF = 2048
D_SSM = 1024
SSM_CH = 16
SSM_GROUPS = 64
SSM_STATE = 64
N_STATE = SSM_GROUPS * SSM_STATE
N_EXPERTS = 256
TOP_K = 8
N_EXPERT_GROUPS = 8
TOPK_GROUPS = 4
D_EXPERT = 512
ROUTE_SCALE = 2.5
EPS = 1e-6
D_IN_PROJ = 4096

MOE_BLK = 128
MOE_SHIFT = MOE_BLK.bit_length() - 1
LANES = 128
SUBLANES = 8
MIB = 1024 * 1024


def _cparams(sem, vmem_mib):
    return pltpu.CompilerParams(dimension_semantics=sem, vmem_limit_bytes=vmem_mib * MIB)


def _rms(x, g):
    return x * lax.rsqrt(jnp.mean(x * x, axis=-1, keepdims=True) + EPS) * g


def _dot(a, b):
    return jnp.dot(a, b, preferred_element_type=F32)


def _dot_nt(a, b):
    return lax.dot_general(a, b, (((1,), (1,)), ((), ())), preferred_element_type=F32)


def _ada_kernel(c_ref, w_ref, b_ref, o_ref):
    a = jax.nn.silu(c_ref[...]).astype(BF16)
    o_ref[...] = _dot(a, w_ref[...].astype(BF16)) + b_ref[...]


def _ada_mod(c_all, w_ada, b_ada):
    rows = c_all.shape[0]
    n_out = w_ada.shape[1]
    tn = 1024
    return pl.pallas_call(
        _ada_kernel,
        grid=(n_out // tn,),
        in_specs=[pl.BlockSpec((rows, D_MODEL), lambda j: (0, 0)),
                  pl.BlockSpec((D_MODEL, tn), lambda j: (0, j)),
                  pl.BlockSpec((1, tn), lambda j: (0, j))],
        out_specs=pl.BlockSpec((rows, tn), lambda j: (0, j)),
        out_shape=jax.ShapeDtypeStruct((rows, n_out), F32),
        compiler_params=_cparams(("arbitrary",), 40),
        name="ada_mod",
    )(c_all, w_ada, b_ada.reshape(1, n_out))


def _rope_kernel(c_ref, s1_ref, s2_ref, *, pos0, stride):
    n = c_ref.shape[0]
    lane = lax.broadcasted_iota(I32, (n, LANES), 1)
    row = lax.broadcasted_iota(I32, (n, LANES), 0) + pl.program_id(0) * n
    pos = (row * stride + pos0).astype(F32)
    half = ROT_DIM // 2
    fi = (lane & (half - 1)).astype(F32)
    inv = jnp.exp(-math.log(ROPE_THETA) * fi / half)
    ang = pos * inv
    cos = jnp.cos(ang)
    sin = jnp.sin(ang)
    c_ref[...] = jnp.where(lane < ROT_DIM, cos, 1.0)
    s1_ref[...] = jnp.where(lane < half, -sin, 0.0)
    s2_ref[...] = jnp.where((lane >= half) & (lane < ROT_DIM), sin, 0.0)


def _rope_tables(n_rows, pos0, stride):
    tr = min(n_rows, 512)
    spec = pl.BlockSpec((tr, LANES), lambda i: (i, 0))
    shp = jax.ShapeDtypeStruct((n_rows, LANES), F32)
    return pl.pallas_call(
        functools.partial(_rope_kernel, pos0=pos0, stride=stride),
        grid=(n_rows // tr,),
        in_specs=[],
        out_specs=[spec, spec, spec],
        out_shape=[shp, shp, shp],
        compiler_params=_cparams(("arbitrary",), 32),
        name="rope_tables",
    )()


def _inproj_kernel(x_ref, sc_ref, sh_ref, ng_ref, w_ref, qkg_ref, rc_ref, rs1_ref, rs2_ref,
                   o_ref, h_scr):
    n = pl.program_id(1)

    @pl.when(n == 0)
    def _():
        h = _rms(x_ref[...], ng_ref[...]) * (1.0 + sc_ref[...]) + sh_ref[...]
        h_scr[...] = h.astype(BF16)

    o_ref[...] = _dot(h_scr[...], w_ref[...])

    @pl.when(n < 2)
    def _():
        g = qkg_ref[n]
        c = rc_ref[...]
        s1 = rs1_ref[...]
        s2 = rs2_ref[...]
        for hh in range(N_HEADS):
            sl = slice(hh * HEAD_DIM, (hh + 1) * HEAD_DIM)
            y = _rms(o_ref[:, sl], g)
            o_ref[:, sl] = (y * c + pltpu.roll(y, HEAD_DIM - ROT_DIM // 2, 1) * s1
                            + pltpu.roll(y, ROT_DIM // 2, 1) * s2)


def _in_proj(x, mod, ng, w_bf, qkg, rope, *, tm, rows_per_mod, rope_blocks):
    t = x.shape[0]
    tn = 1024
    rm = mod.shape[1]
    mt = t // tm

    def mod_map(j):
        return lambda m, n: ((m * tm) // rows_per_mod, 0, j)

    rope_spec = pl.BlockSpec((tm, LANES), lambda m, n: (m % rope_blocks, 0))
    return pl.pallas_call(
        _inproj_kernel,
        grid=(mt, D_IN_PROJ // tn),
        in_specs=[pl.BlockSpec((tm, D_MODEL), lambda m, n: (m, 0)),
                  pl.BlockSpec((None, rm, D_MODEL), mod_map(1)),
                  pl.BlockSpec((None, rm, D_MODEL), mod_map(0)),
                  pl.BlockSpec((1, D_MODEL), lambda m, n: (0, 0)),
                  pl.BlockSpec((D_MODEL, tn), lambda m, n: (0, n)),
                  pl.BlockSpec((2, 1, HEAD_DIM), lambda m, n: (0, 0, 0)),
                  rope_spec, rope_spec, rope_spec],
        out_specs=pl.BlockSpec((tm, tn), lambda m, n: (m, n)),
        out_shape=jax.ShapeDtypeStruct((t, D_IN_PROJ), F32),
        scratch_shapes=[pltpu.VMEM((tm, D_MODEL), BF16)],
        compiler_params=_cparams(("arbitrary", "arbitrary"), 48),
        name="in_proj",
    )(x, mod, mod, ng, w_bf, qkg, *rope)


def _attn_prompt_kernel(q_ref, k_ref, v_ref, o_ref, o0, o1, o2, l0, l1, l2):
    scale = 1.0 / math.sqrt(HEAD_DIM)
    qi = lax.broadcasted_iota(I32, (SPAN, SPAN), 0)
    ki = lax.broadcasted_iota(I32, (SPAN, SPAN), 1)
    own_mask = ki <= qi
    prev_mask = ki >= qi
    neg = -jnp.inf
    oaccs = (o0, o1, o2)
    laccs = (l0, l1, l2)

    for p, d in enumerate(DILATIONS):
        nbk = SEQ // (SPAN * d)
        shift = nbk.bit_length() - 1
        oacc = oaccs[p]
        lacc = laccs[p]

        def rows(start, d=d):
            return pl.ds(start, SPAN) if d == 1 else pl.ds(start, SPAN, stride=d)

        def body(idx, carry, d=d, nbk=nbk, shift=shift, oacc=oacc, lacc=lacc, rows=rows):
            r = idx >> shift
            n = idx & (nbk - 1)
            start = r + n * (SPAN * d)
            pstart = jnp.maximum(start - SPAN * d, r)
            q = q_ref[rows(start), :].astype(BF16)
            ko = k_ref[rows(start), :].astype(BF16)
            vo = v_ref[rows(start), :].astype(BF16)
            kp = k_ref[rows(pstart), :].astype(BF16)
            vp = v_ref[rows(pstart), :].astype(BF16)
            s_o = jnp.where(own_mask, _dot_nt(q, ko) * scale, neg)
            s_p = jnp.where(prev_mask & (n > 0), _dot_nt(q, kp) * scale, neg)
            m = jnp.maximum(jnp.max(s_o, axis=-1, keepdims=True),
                            jnp.max(s_p, axis=-1, keepdims=True))
            p_o = jnp.exp(s_o - m)
            p_p = jnp.exp(s_p - m)
            den = jnp.sum(p_o, axis=-1, keepdims=True) + jnp.sum(p_p, axis=-1, keepdims=True)
            o = (_dot(p_o.astype(BF16), vo) + _dot(p_p.astype(BF16), vp)) / den
            lse = m + jnp.log(den)
            oacc[rows(start), :] = o
            lacc[rows(start), :] = jnp.broadcast_to(lse, (SPAN, HEAD_DIM))
            return carry

        lax.fori_loop(0, SEQ // SPAN, body, 0, unroll=2)

    ch = 512
    for c0 in range(0, SEQ, ch):
        sl = slice(c0, c0 + ch)
        la, lb, lc = l0[sl, :], l1[sl, :], l2[sl, :]
        mx = jnp.maximum(jnp.maximum(la, lb), lc)
        wa, wb, wc = jnp.exp(la - mx), jnp.exp(lb - mx), jnp.exp(lc - mx)
        o_ref[sl, :] = (wa * o0[sl, :] + wb * o1[sl, :] + wc * o2[sl, :]) / (wa + wb + wc)


def _attn_prompt(proj, batch):
    blk = (SEQ, HEAD_DIM)
    scr = pltpu.VMEM((SEQ, HEAD_DIM), F32)
    return pl.pallas_call(
        _attn_prompt_kernel,
        grid=(batch, N_HEADS),
        in_specs=[pl.BlockSpec(blk, lambda b, h: (b, h)),
                  pl.BlockSpec(blk, lambda b, h: (b, N_HEADS + h)),
                  pl.BlockSpec(blk, lambda b, h: (b, 2 * N_HEADS + h))],
        out_specs=pl.BlockSpec(blk, lambda b, h: (b, h)),
        out_shape=jax.ShapeDtypeStruct((batch * SEQ, D_ATTN), F32),
        scratch_shapes=[scr] * 6,
        compiler_params=_cparams(("arbitrary", "arbitrary"), 48),
        name="attn_prompt",
    )(proj, proj, proj)


def _attn_sample_kernel(q_ref, kn_ref, vn_ref, k1, k4, k16, v1, v4, v16, o_ref):
    scale = 1.0 / math.sqrt(HEAD_DIM)
    q = q_ref[...]
    kn = kn_ref[...]
    vn = vn_ref[...]
    s_new = jnp.sum(q * kn, axis=-1, keepdims=True) * scale
    outs = []
    lses = []
    for kc, vc in ((k1, v1), (k4, v4), (k16, v16)):
        kk = kc[...]
        vv = vc[...]
        s = jnp.sum(kk * q[None], axis=-1, keepdims=True) * scale
        m = jnp.maximum(jnp.max(s, axis=0), s_new)
        p = jnp.exp(s - m[None])
        p_new = jnp.exp(s_new - m)
        den = jnp.sum(p, axis=0) + p_new
        o = (jnp.sum(p * vv, axis=0) + p_new * vn) / den
        outs.append(o)
        lses.append(m + jnp.log(den))
    mx = jnp.maximum(jnp.maximum(lses[0], lses[1]), lses[2])
    ws = [jnp.exp(l - mx) for l in lses]
    o_ref[...] = (ws[0] * outs[0] + ws[1] * outs[1] + ws[2] * outs[2]) / (ws[0] + ws[1] + ws[2])


def _attn_sample(q, kn, vn, cache_k, cache_v):
    b = q.shape[0]
    row = pl.BlockSpec((None, N_HEADS, HEAD_DIM), lambda i: (i, 0, 0))
    views, specs = [], []
    for c in (cache_k, cache_v):
        for d in DILATIONS:
            views.append(c.reshape(b, WBUF // d, d, N_HEADS, HEAD_DIM))
            last = WBUF // d // SPAN - 1
            specs.append(pl.BlockSpec((None, SPAN, None, N_HEADS, HEAD_DIM),
                                      lambda i, last=last: (i, last, 0, 0, 0)))
    return pl.pallas_call(
        _attn_sample_kernel,
        grid=(b,),
        in_specs=[row, row, row] + specs,
        out_specs=row,
        out_shape=jax.ShapeDtypeStruct((b, N_HEADS, HEAD_DIM), F32),
        compiler_params=_cparams(("arbitrary",), 32),
        name="attn_sample",
    )(q, kn, vn, *views)


CACHE_CHUNK = 256


def _cache_kernel(c_ref, n_ref, o_ref):
    for r0 in range(0, WBUF - 1, CACHE_CHUNK):
        r1 = min(r0 + CACHE_CHUNK, WBUF - 1)
        o_ref[r0:r1] = c_ref[r0 + 1:r1 + 1]
    o_ref[WBUF - 1] = n_ref[...]


def _cache_update(cache, new):
    b = cache.shape[0]
    blk = pl.BlockSpec((None, WBUF, N_HEADS, HEAD_DIM), lambda i: (i, 0, 0, 0))
    return pl.pallas_call(
        _cache_kernel,
        grid=(b,),
        in_specs=[blk, pl.BlockSpec((None, N_HEADS, HEAD_DIM), lambda i: (i, 0, 0))],
        out_specs=blk,
        out_shape=jax.ShapeDtypeStruct(cache.shape, cache.dtype),
        compiler_params=_cparams(("arbitrary",), 40),
        name="cache_update",
    )(cache, new)


def _ssm_prep_kernel(ar_ref, ai_ref, ldt_ref, br_ref, bi_ref, cr_ref, ci_ref,
                     ab_ref, wb_ref, wc_ref):
    ar = ar_ref[...]
    ai = ai_ref[...]
    dt = jnp.exp(ldt_ref[...])
    mag = jnp.exp(dt * ar)
    abar_re = mag * jnp.cos(dt * ai)
    abar_im = mag * jnp.sin(dt * ai)
    nr = abar_re - 1.0
    den = ar * ar + ai * ai
    coef_re = (nr * ar + abar_im * ai) / den
    coef_im = (abar_im * ar - nr * ai) / den
    ab_ref[0] = abar_re
    ab_ref[1] = abar_im
    wb_ref[...] = jnp.zeros(wb_ref.shape, wb_ref.dtype)
    wc_ref[...] = jnp.zeros(wc_ref.shape, wc_ref.dtype)
    lane = lax.broadcasted_iota(I32, (SSM_CH, LANES), 1)
    gpt = 16
    for g in range(SSM_GROUPS):
        kt, gl = divmod(g, gpt)
        keep = (lane < SSM_STATE) if g % 2 == 0 else (lane >= SSM_STATE)
        cre = coef_re[g:g + 1, :]
        cim = coef_im[g:g + 1, :]
        bre = br_ref[g]
        bim = bi_ref[g]
        bb_re = jnp.where(keep, cre * bre - cim * bim, 0.0).astype(BF16)
        bb_im = jnp.where(keep, cre * bim + cim * bre, 0.0).astype(BF16)
        rs = slice(gl * SSM_CH, (gl + 1) * SSM_CH)
        lt = (gl // 2) * LANES
        wb_ref[kt, rs, lt:lt + LANES] = bb_re
        wb_ref[kt, rs, 1024 + lt:1024 + lt + LANES] = bb_im
        c_re = jnp.where(keep, cr_ref[g], 0.0).astype(BF16)
        c_im = jnp.where(keep, -ci_ref[g], 0.0).astype(BF16)
        wc_ref[kt, rs, lt:lt + LANES] = c_re
        wc_ref[kt, rs, 1024 + lt:1024 + lt + LANES] = c_im


def _ssm_prep(a_re, a_im, log_dt, b_re, b_im, c_re, c_im):
    dup = lambda x: jnp.concatenate([x, x], axis=-1)
    br_t = dup(jnp.swapaxes(b_re, 1, 2))
    bi_t = dup(jnp.swapaxes(b_im, 1, 2))
    shapes = [jax.ShapeDtypeStruct((2, SSM_GROUPS, LANES), F32),
              jax.ShapeDtypeStruct((4, 256, 2048), BF16),
              jax.ShapeDtypeStruct((4, 256, 2048), BF16)]
    ab, wb, wct = pl.pallas_call(
        _ssm_prep_kernel,
        out_shape=shapes,
        compiler_params=pltpu.CompilerParams(vmem_limit_bytes=40 * MIB),
        name="ssm_prep",
    )(dup(a_re), dup(a_im), log_dt.reshape(SSM_GROUPS, 1), br_t, bi_t, dup(c_re), dup(c_im))
    abar = ab[:, :, :SSM_STATE].reshape(2, N_STATE)
    return abar, wb, wct


def _ssm_bu(ub, wb_ref):
    return [_dot(ub[:, kt * 256:(kt + 1) * 256], wb_ref[kt]) for kt in range(4)]


def _ssm_tail(h_re_tiles, h_im_tiles, u, wc_ref, d_ref, wg_ref, bg_ref, og_ref):
    ys = []
    for kt in range(4):
        wct = wc_ref[kt]
        ys.append(_dot_nt(h_re_tiles[kt].astype(BF16), wct[:, :1024])
                  + _dot_nt(h_im_tiles[kt].astype(BF16), wct[:, 1024:]))
    y = jnp.concatenate(ys, axis=-1) + d_ref[...] * u
    g = jax.nn.gelu(y)
    z = _dot(g.astype(BF16), wg_ref[...]) + bg_ref[...]
    out = g * jax.nn.sigmoid(z)
    return _rms(out, og_ref[...])


SCAN_LANES = 256


def _ssm_prompt_kernel(u_ref, wb_ref, wc_ref, ab_ref, d_ref, wg_ref, bg_ref, og_ref,
                       o_ref, hre_ref, him_ref, bre, bim, cre, cim):
    c = pl.program_id(1)
    tc = u_ref.shape[0]

    @pl.when(c == 0)
    def _():
        cre[...] = jnp.zeros(cre.shape, F32)
        cim[...] = jnp.zeros(cim.shape, F32)

    u = u_ref[...]
    bu = _ssm_bu(u.astype(BF16), wb_ref)
    for kt in range(4):
        bre[:, kt * 1024:(kt + 1) * 1024] = bu[kt][:, :1024]
        bim[:, kt * 1024:(kt + 1) * 1024] = bu[kt][:, 1024:]

    row = lax.broadcasted_iota(I32, (SUBLANES, SCAN_LANES), 0)

    def cmul(xr, xi, yr, yi):
        return xr * yr - xi * yi, xr * yi + xi * yr

    for lc in range(N_STATE // SCAN_LANES):
        sl = slice(lc * SCAN_LANES, (lc + 1) * SCAN_LANES)
        a1r = jnp.broadcast_to(ab_ref[0:1, sl], (SUBLANES, SCAN_LANES))
        a1i = jnp.broadcast_to(ab_ref[1:2, sl], (SUBLANES, SCAN_LANES))
        a2r, a2i = cmul(a1r, a1i, a1r, a1i)
        a4r, a4i = cmul(a2r, a2i, a2r, a2i)
        a8r, a8i = cmul(a4r, a4i, a4r, a4i)
        pwr = jnp.ones((SUBLANES, SCAN_LANES), F32)
        pwi = jnp.zeros((SUBLANES, SCAN_LANES), F32)
        for bit, (fr, fi) in ((1, (a1r, a1i)), (2, (a2r, a2i)), (4, (a4r, a4i)), (8, (a8r, a8i))):
            has = ((row + 1) & bit) != 0
            nr, ni = cmul(pwr, pwi, fr, fi)
            pwr = jnp.where(has, nr, pwr)
            pwi = jnp.where(has, ni, pwi)
        steps = ((1, a1r, a1i), (2, a2r, a2i), (4, a4r, a4i))

        def body(m, carry, sl=sl, steps=steps, pwr=pwr, pwi=pwi):
            cr, ci = carry
            off = pl.multiple_of(m * SUBLANES, SUBLANES)
            xr = bre[pl.ds(off, SUBLANES), sl]
            xi = bim[pl.ds(off, SUBLANES), sl]
            for s, fr, fi in steps:
                sr = jnp.where(row >= s, pltpu.roll(xr, s, 0), 0.0)
                si = jnp.where(row >= s, pltpu.roll(xi, s, 0), 0.0)
                tr, ti = cmul(fr, fi, sr, si)
                xr = xr + tr
                xi = xi + ti
            tr, ti = cmul(pwr, pwi, cr, ci)
            xr = xr + tr
            xi = xi + ti
            bre[pl.ds(off, SUBLANES), sl] = xr
            bim[pl.ds(off, SUBLANES), sl] = xi
            cr = jnp.broadcast_to(xr[SUBLANES - 1:SUBLANES, :], (SUBLANES, SCAN_LANES))
            ci = jnp.broadcast_to(xi[SUBLANES - 1:SUBLANES, :], (SUBLANES, SCAN_LANES))
            return cr, ci

        cr, ci = lax.fori_loop(0, tc // SUBLANES, body, (cre[:, sl], cim[:, sl]), unroll=2)
        cre[:, sl] = cr
        cim[:, sl] = ci

    h_re = [bre[:, kt * 1024:(kt + 1) * 1024] for kt in range(4)]
    h_im = [bim[:, kt * 1024:(kt + 1) * 1024] for kt in range(4)]
    o_ref[...] = _ssm_tail(h_re, h_im, u, wc_ref, d_ref, wg_ref, bg_ref, og_ref).astype(BF16)

    @pl.when(c == pl.num_programs(1) - 1)
    def _():
        hre_ref[...] = cre[0:1, :]
        him_ref[...] = cim[0:1, :]


def _ssm_prompt(proj, batch, abar, wb, wct, ssm_d, wglu_bf, bglu, og):
    tc = 256
    nchunk = SEQ // tc
    const2 = lambda shape: pl.BlockSpec(shape, lambda b, c: (0,) * len(shape))
    st_spec = pl.BlockSpec((None, 1, N_STATE), lambda b, c: (b, 0, 0))
    st_shape = jax.ShapeDtypeStruct((batch, 1, N_STATE), F32)
    return pl.pallas_call(
        _ssm_prompt_kernel,
        grid=(batch, nchunk),
        in_specs=[pl.BlockSpec((tc, D_SSM), lambda b, c: (b * nchunk + c, 3)),
                  const2((4, 256, 2048)), const2((4, 256, 2048)), const2((2, N_STATE)),
                  const2((1, D_SSM)), const2((D_SSM, D_SSM)), const2((1, D_SSM)), const2((1, D_SSM))],
        out_specs=[pl.BlockSpec((tc, D_SSM), lambda b, c: (b * nchunk + c, 0)), st_spec, st_spec],
        out_shape=[jax.ShapeDtypeStruct((batch * SEQ, D_SSM), BF16), st_shape, st_shape],
        scratch_shapes=[pltpu.VMEM((tc, N_STATE), F32), pltpu.VMEM((tc, N_STATE), F32),
                        pltpu.VMEM((SUBLANES, N_STATE), F32), pltpu.VMEM((SUBLANES, N_STATE), F32)],
        compiler_params=_cparams(("arbitrary", "arbitrary"), 48),
        name="ssm_prompt",
    )(proj, wb, wct, abar, ssm_d, wglu_bf, bglu, og)


def _ssm_sample_kernel(u_ref, h0r_ref, h0i_ref, wb_ref, wc_ref, ab_ref, d_ref, wg_ref, bg_ref, og_ref,
                       o_ref, hre_ref, him_ref):
    u = u_ref[...]
    bu = _ssm_bu(u.astype(BF16), wb_ref)
    h_re, h_im = [], []
    for kt in range(4):
        sl = slice(kt * 1024, (kt + 1) * 1024)
        ar = ab_ref[0:1, sl]
        ai = ab_ref[1:2, sl]
        h0r = h0r_ref[:, sl]
        h0i = h0i_ref[:, sl]
        hr = bu[kt][:, :1024] + (ar * h0r - ai * h0i)
        hi = bu[kt][:, 1024:] + (ar * h0i + ai * h0r)
        hre_ref[:, sl] = hr
        him_ref[:, sl] = hi
        h_re.append(hr)
        h_im.append(hi)
    o_ref[...] = _ssm_tail(h_re, h_im, u, wc_ref, d_ref, wg_ref, bg_ref, og_ref).astype(BF16)


def _ssm_sample(u, h0_re, h0_im, abar, wb, wct, ssm_d, wglu_bf, bglu, og):
    b = u.shape[0]
    st_shape = jax.ShapeDtypeStruct((b, N_STATE), F32)
    return pl.pallas_call(
        _ssm_sample_kernel,
        out_shape=[jax.ShapeDtypeStruct((b, D_SSM), BF16), st_shape, st_shape],
        compiler_params=pltpu.CompilerParams(vmem_limit_bytes=40 * MIB),
        name="ssm_sample",
    )(u, h0_re, h0_im, wb, wct, abar, ssm_d, wglu_bf, bglu, og)


def _outproj_kernel(attn_ref, ssm_ref, ag_ref, w_ref, x_ref, g1_ref, o_ref, lhs):
    n = pl.program_id(1)

    @pl.when(n == 0)
    def _():
        lhs[:, :D_ATTN] = _rms(attn_ref[...], ag_ref[...]).astype(BF16)
        lhs[:, D_ATTN:] = ssm_ref[...]

    o_ref[...] = x_ref[...] + g1_ref[...] * _dot(lhs[...], w_ref[...])


def _out_proj(attn, ssm_n, ag, w_bf, x, mod, *, tm, rows_per_mod):
    t = x.shape[0]
    tn = 1024
    rm = mod.shape[1]
    return pl.pallas_call(
        _outproj_kernel,
        grid=(t // tm, D_MODEL // tn),
        in_specs=[pl.BlockSpec((tm, D_ATTN), lambda m, n: (m, 0)),
                  pl.BlockSpec((tm, D_SSM), lambda m, n: (m, 0)),
                  pl.BlockSpec((1, D_ATTN), lambda m, n: (0, 0)),
                  pl.BlockSpec((D_MODEL, tn), lambda m, n: (0, n)),
                  pl.BlockSpec((tm, tn), lambda m, n: (m, n)),
                  pl.BlockSpec((None, rm, tn), lambda m, n: ((m * tm) // rows_per_mod, 0, 2 * (D_MODEL // tn) + n))],
        out_specs=pl.BlockSpec((tm, tn), lambda m, n: (m, n)),
        out_shape=jax.ShapeDtypeStruct((t, D_MODEL), F32),
        scratch_shapes=[pltpu.VMEM((tm, D_MODEL), BF16)],
        compiler_params=_cparams(("arbitrary", "arbitrary"), 48),
        name="out_proj",
    )(attn, ssm_n, ag, w_bf, x, mod)


def _route_kernel(x_ref, sc_ref, sh_ref, ng_ref, wr_ref, br_ref, h2_ref, te_ref, gt_ref):
    tm = x_ref.shape[0]
    h2 = _rms(x_ref[...], ng_ref[...]) * (1.0 + sc_ref[...]) + sh_ref[...]
    h2_ref[...] = h2
    scores = jax.nn.sigmoid(_dot_nt(wr_ref[...], h2.astype(BF16)))
    biased = scores + br_ref[...]
    neg = -jnp.inf
    per_g = N_EXPERTS // N_EXPERT_GROUPS
    gi_iota = lax.broadcasted_iota(I32, (per_g, tm), 0)
    gscore = []
    for g in range(N_EXPERT_GROUPS):
        xg = biased[g * per_g:(g + 1) * per_g, :]
        m1 = jnp.max(xg, axis=0, keepdims=True)
        i1 = jnp.min(jnp.where(xg == m1, gi_iota, per_g), axis=0, keepdims=True)
        m2 = jnp.max(jnp.where(gi_iota == i1, neg, xg), axis=0, keepdims=True)
        gscore.append(m1 + m2)
    parts = []
    for i in range(N_EXPERT_GROUPS):
        rank = jnp.zeros((1, tm), I32)
        for j in range(N_EXPERT_GROUPS):
            if j == i:
                continue
            ahead = (gscore[j] >= gscore[i]) if j < i else (gscore[j] > gscore[i])
            rank = rank + ahead.astype(I32)
        parts.append(jnp.where(rank < TOPK_GROUPS, biased[i * per_g:(i + 1) * per_g, :], neg))
    cur = jnp.concatenate(parts, axis=0)
    e_iota = lax.broadcasted_iota(I32, (N_EXPERTS, tm), 0)
    idxs, gates = [], []
    for _ in range(TOP_K):
        m = jnp.max(cur, axis=0, keepdims=True)
        idx = jnp.min(jnp.where(cur == m, e_iota, N_EXPERTS), axis=0, keepdims=True)
        hit = e_iota == idx
        gates.append(jnp.sum(jnp.where(hit, scores, 0.0), axis=0, keepdims=True))
        idxs.append(idx)
        cur = jnp.where(hit, neg, cur)
    tot = gates[0]
    for g in gates[1:]:
        tot = tot + g
    for k in range(TOP_K):
        te_ref[k:k + 1, :] = idxs[k]
        gt_ref[k:k + 1, :] = gates[k] / tot * ROUTE_SCALE


def _route(x1, mod, ng, wr_t_bf, b_router, *, tm, rows_per_mod):
    t = x1.shape[0]
    rm = mod.shape[1]

    def mod_map(j):
        return lambda m: ((m * tm) // rows_per_mod, 0, j)

    return pl.pallas_call(
        _route_kernel,
        grid=(t // tm,),
        in_specs=[pl.BlockSpec((tm, D_MODEL), lambda m: (m, 0)),
                  pl.BlockSpec((None, rm, D_MODEL), mod_map(4)),
                  pl.BlockSpec((None, rm, D_MODEL), mod_map(3)),
                  pl.BlockSpec((1, D_MODEL), lambda m: (0, 0)),
                  pl.BlockSpec((N_EXPERTS, D_MODEL), lambda m: (0, 0)),
                  pl.BlockSpec((N_EXPERTS, 1), lambda m: (0, 0))],
        out_specs=[pl.BlockSpec((tm, D_MODEL), lambda m: (m, 0)),
                   pl.BlockSpec((TOP_K, tm), lambda m: (0, m)),
                   pl.BlockSpec((TOP_K, tm), lambda m: (0, m))],
        out_shape=[jax.ShapeDtypeStruct((t, D_MODEL), F32),
                   jax.ShapeDtypeStruct((TOP_K, t), I32),
                   jax.ShapeDtypeStruct((TOP_K, t), F32)],
        compiler_params=_cparams(("arbitrary",), 40),
        name="route",
    )(x1, mod, mod, ng, wr_t_bf, b_router.reshape(N_EXPERTS, 1))


DISP_TM = 256


def _rank_kernel(te_ref, rank_ref, cnt_ref, run):
    i = pl.program_id(0)
    tm = te_ref.shape[1]

    @pl.when(i == 0)
    def _():
        run[...] = jnp.zeros(run.shape, F32)

    te = te_ref[...]
    e_iota = lax.broadcasted_iota(I32, (N_EXPERTS, tm), 0)
    a = jnp.zeros((N_EXPERTS, tm), F32)
    for k in range(TOP_K):
        a = a + (e_iota == te[k:k + 1, :]).astype(F32)
    upper = (lax.broadcasted_iota(I32, (tm, tm), 0) < lax.broadcasted_iota(I32, (tm, tm), 1)).astype(BF16)
    tot = _dot(a.astype(BF16), upper) + run[:, 0:1]
    for k in range(TOP_K):
        rk = jnp.sum(jnp.where(e_iota == te[k:k + 1, :], tot, 0.0), axis=0, keepdims=True)
        rank_ref[k:k + 1, :] = rk.astype(I32)
    run[...] = run[...] + jnp.sum(a, axis=1, keepdims=True)

    @pl.when(i == pl.num_programs(0) - 1)
    def _():
        cnt_ref[...] = run[...].astype(I32)


def _dest_kernel(te_ref, rank_ref, cnt_ref, dest_ref, ps_ref, nblk_ref, ps_scr):
    i = pl.program_id(0)
    tm = te_ref.shape[1]

    @pl.when(i == 0)
    def _():
        nb_e = ((cnt_ref[...] + (MOE_BLK - 1)) >> MOE_SHIFT).astype(F32)
        lower = (lax.broadcasted_iota(I32, (N_EXPERTS, N_EXPERTS), 1)
                 < lax.broadcasted_iota(I32, (N_EXPERTS, N_EXPERTS), 0)).astype(BF16)
        ps_b = _dot(lower, nb_e.astype(BF16))
        ps_scr[...] = ps_b * MOE_BLK
        ps_ref[...] = (ps_b * MOE_BLK).astype(I32)
        nblk_ref[...] = jnp.max(ps_b + nb_e, axis=0, keepdims=True).astype(I32)

    te = te_ref[...]
    rk = rank_ref[...]
    ps = ps_scr[:, 0:1]
    e_iota = lax.broadcasted_iota(I32, (N_EXPERTS, tm), 0)
    for k in range(TOP_K):
        tek = te[k:k + 1, :]
        base = jnp.sum(jnp.where(e_iota == tek, ps, 0.0), axis=0, keepdims=True)
        dest_ref[k:k + 1, :] = jnp.where(tek >= 0, base.astype(I32) + rk[k:k + 1, :], -1)


def _dispatch_tables(te):
    tpad = te.shape[1]
    tm = DISP_TM
    tile = pl.BlockSpec((TOP_K, tm), lambda i: (0, i))
    full = lambda shape: pl.BlockSpec(shape, lambda i: (0,) * len(shape))
    rank, cnt = pl.pallas_call(
        _rank_kernel,
        grid=(tpad // tm,),
        in_specs=[tile],
        out_specs=[tile, full((N_EXPERTS, LANES))],
        out_shape=[jax.ShapeDtypeStruct((TOP_K, tpad), I32),
                   jax.ShapeDtypeStruct((N_EXPERTS, LANES), I32)],
        scratch_shapes=[pltpu.VMEM((N_EXPERTS, LANES), F32)],
        compiler_params=_cparams(("arbitrary",), 32),
        name="moe_rank",
    )(te)
    dest, ps, nblk = pl.pallas_call(
        _dest_kernel,
        grid=(tpad // tm,),
        in_specs=[tile, tile, full((N_EXPERTS, LANES))],
        out_specs=[tile, full((N_EXPERTS, LANES)), full((1, LANES))],
        out_shape=[jax.ShapeDtypeStruct((TOP_K, tpad), I32),
                   jax.ShapeDtypeStruct((N_EXPERTS, LANES), I32),
                   jax.ShapeDtypeStruct((1, LANES), I32)],
        scratch_shapes=[pltpu.VMEM((N_EXPERTS, LANES), F32)],
        compiler_params=_cparams(("arbitrary",), 32),
        name="moe_dest",
    )(te, rank, cnt)
    return dest, nblk[0, :1], cnt[:, 0], ps[:, 0]


PAD_BITS = (64, 32, 16, 8)


def _scatter_kernel(cnt_ref, ps_ref, nblk_ref, dest_ref, hp_ref, hs_ref, xs_ref, zeros, sem, zsem, *,
                    n_prompt_tiles):
    i = pl.program_id(0)
    tm = dest_ref.shape[1]

    def row_copy(src_ref, r, d):
        return pltpu.make_async_copy(src_ref.at[pl.ds(r, 1)], xs_ref.at[pl.ds(d, 1)], sem)

    def scatter_rows(src_ref, n_rows):
        def issue(r, c):
            for k in range(TOP_K):
                row_copy(src_ref, r, dest_ref[k, r]).start()
            return c

        def drain(r, c):
            for k in range(TOP_K):
                row_copy(src_ref, r, dest_ref[k, r]).wait()
            return c

        lax.fori_loop(0, n_rows, issue, 0, unroll=2)
        lax.fori_loop(0, n_rows, drain, 0, unroll=2)

    @pl.when(i < n_prompt_tiles)
    def _():
        scatter_rows(hp_ref, tm)

    @pl.when(i == n_prompt_tiles)
    def _():
        scatter_rows(hs_ref, hs_ref.shape[0])

    @pl.when(i == 0)
    def _():
        zeros[...] = jnp.zeros(zeros.shape, F32)

        def pad_copies(e, do):
            n = cnt_ref[e]
            npad = (((n + (MOE_BLK - 1)) >> MOE_SHIFT) << MOE_SHIFT) - n
            base = ps_ref[e] + n
            head = npad & (SUBLANES - 1)
            for s in range(SUBLANES - 1):
                @pl.when(s < head)
                def _(s=s):
                    do(pltpu.make_async_copy(zeros.at[pl.ds(0, 1)], xs_ref.at[pl.ds(base + s, 1)], zsem))
            off = base + head
            for bit in PAD_BITS:
                take = (npad & bit) != 0

                @pl.when(take)
                def _(off=off, bit=bit):
                    dst = xs_ref.at[pl.ds(pl.multiple_of(off, SUBLANES), bit)]
                    do(pltpu.make_async_copy(zeros.at[pl.ds(0, bit)], dst, zsem))
                off = off + jnp.where(take, bit, 0)

        def issue(e, c):
            pad_copies(e, lambda cp: cp.start())
            return c

        def drain(e, c):
            pad_copies(e, lambda cp: cp.wait())
            return c

        lax.fori_loop(0, N_EXPERTS, issue, 0)
        lax.fori_loop(0, N_EXPERTS, drain, 0)

        def tail_copy(b):
            dst = xs_ref.at[pl.ds(pl.multiple_of(b * MOE_BLK, MOE_BLK), MOE_BLK)]
            return pltpu.make_async_copy(zeros, dst, zsem)

        def tail_issue(b, c):
            @pl.when(b >= nblk_ref[0])
            def _():
                tail_copy(b).start()
            return c

        def tail_drain(b, c):
            @pl.when(b >= nblk_ref[0])
            def _():
                tail_copy(b).wait()
            return c

        n_blocks = xs_ref.shape[0] // MOE_BLK
        lax.fori_loop(0, n_blocks, tail_issue, 0)
        lax.fori_loop(0, n_blocks, tail_drain, 0)


def _scatter_tokens(dest, cnt, ps, nblk, h2_p, h2_s, n_slots):
    tm = DISP_TM
    n_prompt_tiles = h2_p.shape[0] // tm
    grid_spec = pltpu.PrefetchScalarGridSpec(
        num_scalar_prefetch=3,
        grid=(n_prompt_tiles + 1,),
        in_specs=[pl.BlockSpec((TOP_K, tm), lambda i, *_: (0, i), memory_space=pltpu.SMEM),
                  pl.BlockSpec((tm, D_MODEL), lambda i, *_: (jnp.minimum(i, n_prompt_tiles - 1), 0)),
                  pl.BlockSpec(h2_s.shape, lambda i, *_: (0, 0))],
        out_specs=pl.BlockSpec(memory_space=pl.ANY),
        scratch_shapes=[pltpu.VMEM((MOE_BLK, D_MODEL), F32),
                        pltpu.SemaphoreType.DMA(()), pltpu.SemaphoreType.DMA(())],
    )
    return pl.pallas_call(
        functools.partial(_scatter_kernel, n_prompt_tiles=n_prompt_tiles),
        grid_spec=grid_spec,
        out_shape=jax.ShapeDtypeStruct((n_slots, D_MODEL), F32),
        compiler_params=_cparams(("arbitrary",), 32),
        name="moe_scatter",
    )(cnt, ps, nblk, dest, h2_p, h2_s)


def _moe_kernel(cnt_ref, ps_ref, nblk_ref, wg_ref, wu_ref, wd_ref, xs_ref, ys_ref,
                wgb, wub, wdb, xbuf, ybuf, xsem, ysem):
    e = pl.program_id(0)
    total = nblk_ref[0]
    nb = (cnt_ref[e] + (MOE_BLK - 1)) >> MOE_SHIFT
    g0 = ps_ref[e] >> MOE_SHIFT

    def rows(g):
        return pl.ds(pl.multiple_of(g * MOE_BLK, MOE_BLK), MOE_BLK)

    def x_copy(g, slot):
        return pltpu.make_async_copy(xs_ref.at[rows(g)], xbuf.at[slot], xsem.at[slot])

    def y_copy(g, slot):
        return pltpu.make_async_copy(ybuf.at[slot], ys_ref.at[rows(g)], ysem.at[slot])

    @pl.when(e == 0)
    def _():
        x_copy(0, 0).start()

    @pl.when(nb > 0)
    def _():
        wgb[...] = wg_ref[...].astype(BF16)
        wub[...] = wu_ref[...].astype(BF16)
        wdb[...] = wd_ref[...].astype(BF16)

        def body(c, carry):
            g = g0 + c
            slot = g & 1

            @pl.when(g + 1 < total)
            def _():
                x_copy(g + 1, 1 - slot).start()

            x_copy(g, slot).wait()

            @pl.when(g >= 2)
            def _():
                y_copy(g - 2, slot).wait()

            x = xbuf[slot].astype(BF16)
            hid = jax.nn.silu(_dot(x, wgb[...])) * _dot(x, wub[...])
            ybuf[slot] = _dot(hid.astype(BF16), wdb[...])
            y_copy(g, slot).start()
            return carry

        lax.fori_loop(0, nb, body, 0)

    @pl.when(e == pl.num_programs(0) - 1)
    def _():
        @pl.when(total >= 2)
        def _():
            y_copy(total - 2, total & 1).wait()

        y_copy(total - 1, (total - 1) & 1).wait()

        ybuf[0] = jnp.zeros((MOE_BLK, D_MODEL), F32)

        def tail_issue(b, c):
            @pl.when(b >= total)
            def _():
                y_copy(b, 0).start()
            return c

        def tail_drain(b, c):
            @pl.when(b >= total)
            def _():
                y_copy(b, 0).wait()
            return c

        n_blocks = ys_ref.shape[0] // MOE_BLK
        lax.fori_loop(0, n_blocks, tail_issue, 0)
        lax.fori_loop(0, n_blocks, tail_drain, 0)


def _moe_experts(cnt, ps, nblk, xs, w_gate, w_up, w_down):
    n_slots = xs.shape[0]
    w_map = lambda e, *_: (e, 0, 0)
    anyspec = pl.BlockSpec(memory_space=pl.ANY)
    grid_spec = pltpu.PrefetchScalarGridSpec(
        num_scalar_prefetch=3,
        grid=(N_EXPERTS,),
        in_specs=[pl.BlockSpec((None, D_MODEL, D_EXPERT), w_map),
                  pl.BlockSpec((None, D_MODEL, D_EXPERT), w_map),
                  pl.BlockSpec((None, D_EXPERT, D_MODEL), w_map),
                  anyspec],
        out_specs=anyspec,
        scratch_shapes=[pltpu.VMEM((D_MODEL, D_EXPERT), BF16), pltpu.VMEM((D_MODEL, D_EXPERT), BF16),
                        pltpu.VMEM((D_EXPERT, D_MODEL), BF16),
                        pltpu.VMEM((2, MOE_BLK, D_MODEL), F32), pltpu.VMEM((2, MOE_BLK, D_MODEL), F32),
                        pltpu.SemaphoreType.DMA((2,)), pltpu.SemaphoreType.DMA((2,))],
    )
    return pl.pallas_call(
        _moe_kernel,
        grid_spec=grid_spec,
        out_shape=jax.ShapeDtypeStruct((n_slots, D_MODEL), F32),
        compiler_params=_cparams(("arbitrary",), 48),
        name="moe_experts",
    )(cnt, ps, nblk, w_gate, w_up, w_down, xs)


def _combine_kernel(dcur_ref, dnxt_ref, gate_ref, h2_ref, x1_ref, g2_ref, wsg_ref, wsu_ref, wsd_ref,
                    ys_ref, o_ref, ybuf, sem):
    i = pl.program_id(0)
    n = pl.num_programs(0)
    tm = h2_ref.shape[0]
    slot = i % 2

    def row_copy(d, s, k, r):
        return pltpu.make_async_copy(ys_ref.at[pl.ds(d, 1)], ybuf.at[s, k, pl.ds(r, 1)], sem.at[s])

    def gather(dref, s, wait):
        def step(r, c):
            for k in range(TOP_K):
                cp = row_copy(dref[k, r], s, k, r)
                if wait:
                    cp.wait()
                else:
                    cp.start()
            return c

        lax.fori_loop(0, tm, step, 0, unroll=2)

    @pl.when(i == 0)
    def _():
        gather(dcur_ref, 0, False)

    for par in (0, 1):
        @pl.when((slot == par) & (i + 1 < n))
        def _(par=par):
            gather(dnxt_ref, 1 - par, False)

    for par in (0, 1):
        @pl.when(slot == par)
        def _(par=par):
            gather(dcur_ref, par, True)

    gates = gate_ref[...]
    routed = gates[:, 0:1] * ybuf[slot, 0]
    for k in range(1, TOP_K):
        routed = routed + gates[:, k:k + 1] * ybuf[slot, k]
    hb = h2_ref[...].astype(BF16)
    hid = jax.nn.silu(_dot(hb, wsg_ref[...])) * _dot(hb, wsu_ref[...])
    shared = _dot(hid.astype(BF16), wsd_ref[...])
    o_ref[...] = x1_ref[...] + g2_ref[...] * (routed + shared)


def _combine(dest, gate_tk, h2, x1, mod, wsg_bf, wsu_bf, wsd_bf, ys, *, tm, rows_per_mod):
    t = h2.shape[0]
    nt = t // tm
    rm = mod.shape[1]
    const = lambda shape: pl.BlockSpec(shape, lambda i: (0,) * len(shape))
    return pl.pallas_call(
        _combine_kernel,
        grid=(nt,),
        in_specs=[pl.BlockSpec((TOP_K, tm), lambda i: (0, i), memory_space=pltpu.SMEM),
                  pl.BlockSpec((TOP_K, tm), lambda i: (0, jnp.minimum(i + 1, nt - 1)), memory_space=pltpu.SMEM),
                  pl.BlockSpec((tm, TOP_K), lambda i: (i, 0)),
                  pl.BlockSpec((tm, D_MODEL), lambda i: (i, 0)),
                  pl.BlockSpec((tm, D_MODEL), lambda i: (i, 0)),
                  pl.BlockSpec((None, rm, D_MODEL), lambda i: ((i * tm) // rows_per_mod, 0, 5)),
                  const((D_MODEL, D_EXPERT)), const((D_MODEL, D_EXPERT)), const((D_EXPERT, D_MODEL)),
                  pl.BlockSpec(memory_space=pl.ANY)],
        out_specs=pl.BlockSpec((tm, D_MODEL), lambda i: (i, 0)),
        out_shape=jax.ShapeDtypeStruct((t, D_MODEL), F32),
        scratch_shapes=[pltpu.VMEM((2, TOP_K, tm, D_MODEL), F32), pltpu.SemaphoreType.DMA((2,))],
        compiler_params=_cparams(("arbitrary",), 48),
        name="moe_combine",
    )(dest, dest, gate_tk, h2, x1, mod, wsg_bf, wsu_bf, wsd_bf, ys)


def kernel(x_prompt, x_sample, cache_k, cache_v, state_ssm_re, state_ssm_im, c_prompt, c_sample,
           w_ada, b_ada, norm1_g, w_in, q_norm_g, k_norm_g, ssm_A_re, ssm_A_im, ssm_log_dt,
           ssm_B_re, ssm_B_im, ssm_C_re, ssm_C_im, ssm_D, ssm_w_glu, ssm_b_glu, attn_out_g,
           ssm_out_g, w_out, norm2_g, w_router, b_router, w_exp_gate, w_exp_up, w_exp_down,
           w_sh_gate, w_sh_up, w_sh_down):
    nb_p, seq, _ = x_prompt.shape
    nb_s = x_sample.shape[0]
    t_p = nb_p * seq
    l = 0

    n_c = nb_p + nb_s
    c_rows = -(-n_c // SUBLANES) * SUBLANES
    c_all = jnp.concatenate([c_prompt, c_sample, jnp.zeros((c_rows - n_c, D_MODEL), F32)], axis=0)
    mod = _ada_mod(c_all, w_ada[l], b_ada[l])
    mod_p = mod[:nb_p].reshape(nb_p, 1, 6 * D_MODEL)
    mod_s = mod[nb_p:n_c].reshape(1, nb_s, 6 * D_MODEL)

    w_in_bf = w_in[l].astype(BF16)
    w_out_bf = w_out[l].astype(BF16)
    wglu_bf = ssm_w_glu[l].astype(BF16)
    wr_t_bf = w_router[l].T.astype(BF16)
    wsg_bf = w_sh_gate[l].astype(BF16)
    wsu_bf = w_sh_up[l].astype(BF16)
    wsd_bf = w_sh_down[l].astype(BF16)
    ng1 = norm1_g[l].reshape(1, D_MODEL)
    ng2 = norm2_g[l].reshape(1, D_MODEL)
    qkg = jnp.stack([q_norm_g[l], k_norm_g[l]]).reshape(2, 1, HEAD_DIM)
    ag = attn_out_g[l].reshape(1, D_ATTN)
    og = ssm_out_g[l].reshape(1, D_SSM)
    ssm_d = ssm_D[l].reshape(1, D_SSM)
    bglu = ssm_b_glu[l].reshape(1, D_SSM)

    xp = x_prompt.reshape(t_p, D_MODEL)
    xs = x_sample.reshape(nb_s, D_MODEL)

    tm_p = 512
    rope_p = _rope_tables(seq, 0, 1)
    rope_s = _rope_tables(nb_s, PAST_LEN, 0)
    proj_p = _in_proj(xp, mod_p, ng1, w_in_bf, qkg, rope_p, tm=tm_p, rows_per_mod=seq,
                      rope_blocks=seq // tm_p)
    proj_s = _in_proj(xs, mod_s, ng1, w_in_bf, qkg, rope_s, tm=nb_s, rows_per_mod=nb_s, rope_blocks=1)

    attn_p = _attn_prompt(proj_p, nb_p)
    q_s = proj_s[:, :D_ATTN].reshape(nb_s, N_HEADS, HEAD_DIM)
    k_s = proj_s[:, D_ATTN:2 * D_ATTN].reshape(nb_s, N_HEADS, HEAD_DIM)
    v_s = proj_s[:, 2 * D_ATTN:3 * D_ATTN].reshape(nb_s, N_HEADS, HEAD_DIM)
    attn_s = _attn_sample(q_s, k_s, v_s, cache_k[l], cache_v[l]).reshape(nb_s, D_ATTN)
    k_new = _cache_update(cache_k[l], k_s)
    v_new = _cache_update(cache_v[l], v_s)

    abar, wb, wct = _ssm_prep(ssm_A_re[l], ssm_A_im[l], ssm_log_dt[l], ssm_B_re[l], ssm_B_im[l],
                              ssm_C_re[l], ssm_C_im[l])
    ssm_p, hre_p, him_p = _ssm_prompt(proj_p, nb_p, abar, wb, wct, ssm_d, wglu_bf, bglu, og)
    ssm_s, hre_s, him_s = _ssm_sample(proj_s[:, 3 * D_ATTN:], state_ssm_re[l].reshape(nb_s, N_STATE),
                                      state_ssm_im[l].reshape(nb_s, N_STATE), abar, wb, wct,
                                      ssm_d, wglu_bf, bglu, og)

    x1_p = _out_proj(attn_p, ssm_p, ag, w_out_bf, xp, mod_p, tm=tm_p, rows_per_mod=seq)
    x1_s = _out_proj(attn_s, ssm_s, ag, w_out_bf, xs, mod_s, tm=nb_s, rows_per_mod=nb_s)

    h2_p, te_p, gt_p = _route(x1_p, mod_p, ng2, wr_t_bf, b_router[l], tm=256, rows_per_mod=seq)
    h2_s, te_s, gt_s = _route(x1_s, mod_s, ng2, wr_t_bf, b_router[l], tm=nb_s, rows_per_mod=nb_s)
    t_pad = (t_p // DISP_TM + 1) * DISP_TM
    te = jnp.concatenate([te_p, te_s, jnp.full((TOP_K, t_pad - t_p - nb_s), -1, I32)], axis=1)
    dest, nblk, cnt, ps = _dispatch_tables(te)
    n_assign = (t_p + nb_s) * TOP_K
    n_slots = (-(-n_assign // MOE_BLK) + N_EXPERTS) * MOE_BLK
    xs_sorted = _scatter_tokens(dest, cnt, ps, nblk, h2_p, h2_s, n_slots)
    ys_sorted = _moe_experts(cnt, ps, nblk, xs_sorted, w_exp_gate[l], w_exp_up[l], w_exp_down[l])
    y_p = _combine(dest[:, :t_p], gt_p.T, h2_p, x1_p, mod_p, wsg_bf, wsu_bf, wsd_bf, ys_sorted,
                   tm=128, rows_per_mod=seq)
    y_s = _combine(dest[:, t_p:t_p + nb_s], gt_s.T, h2_s, x1_s, mod_s, wsg_bf, wsu_bf, wsd_bf, ys_sorted,
                   tm=nb_s, rows_per_mod=nb_s)

    keep = min(WBUF, seq)
    kv = proj_p.reshape(nb_p, seq, D_IN_PROJ)[:, seq - keep:, D_ATTN:3 * D_ATTN]
    k_p = kv[..., :D_ATTN].reshape(1, nb_p, keep, N_HEADS, HEAD_DIM)
    v_p = kv[..., D_ATTN:].reshape(1, nb_p, keep, N_HEADS, HEAD_DIM)
    st = lambda h, b: h.reshape(1, b, SSM_GROUPS, SSM_STATE)
    return (y_p.reshape(nb_p, seq, D_MODEL), y_s.reshape(nb_s, 1, D_MODEL), k_p, v_p,
            st(hre_p, nb_p), st(him_p, nb_p), k_new[None], v_new[None], st(hre_s, nb_s), st(him_s, nb_s))
```

```python
import functools
import math

import jax
import jax.numpy as jnp
from jax import lax
from jax.experimental import pallas as pl
from jax.experimental.pallas import tpu as pltpu

F32 = jnp.float32
BF16 = jnp.bfloat16
I32 = jnp.int32

D_MODEL = 2048
SEQ = 4096
PAST_LEN = 8192
D_ATTN = 1024
HEAD_DIM = 128
N_HEADS = 8
ROT_DIM = 32
ROPE_THETA = 500000.0
DILATIONS = (1, 4, 16)
SPAN = 128
WBUF = 2048
D_SSM = 1024
SSM_CH = 16
SSM_GROUPS = 64
SSM_STATE = 64
N_STATE = SSM_GROUPS * SSM_STATE
N_EXPERTS = 256
TOP_K = 8
N_EXPERT_GROUPS = 8
TOPK_GROUPS = 4
D_EXPERT = 512
ROUTE_SCALE = 2.5
EPS = 1e-6
D_IN_PROJ = 4096

MOE_BLK = 128
MOE_SHIFT = MOE_BLK.bit_length() - 1
LANES = 128
SUBLANES = 8
MIB = 1024 * 1024


def _cparams(sem, vmem_mib):
    return pltpu.CompilerParams(dimension_semantics=sem, vmem_limit_bytes=vmem_mib * MIB)


def _rms(x, g):
    return x * lax.rsqrt(jnp.mean(x * x, axis=-1, keepdims=True) + EPS) * g


def _dot(a, b):
    return jnp.dot(a, b, preferred_element_type=F32)


def _dot_nt(a, b):
    return lax.dot_general(a, b, (((1,), (1,)), ((), ())), preferred_element_type=F32)


def _ada_kernel(c_ref, w_ref, b_ref, o_ref):
    a = jax.nn.silu(c_ref[...]).astype(BF16)
    o_ref[...] = _dot(a, w_ref[...].astype(BF16)) + b_ref[...]


def _ada_mod(c_all, w_ada, b_ada):
    rows = c_all.shape[0]
    n_out = w_ada.shape[1]
    tn = 1024
    return pl.pallas_call(
        _ada_kernel,
        grid=(n_out // tn,),
        in_specs=[pl.BlockSpec((rows, D_MODEL), lambda j: (0, 0)),
                  pl.BlockSpec((D_MODEL, tn), lambda j: (0, j)),
                  pl.BlockSpec((1, tn), lambda j: (0, j))],
        out_specs=pl.BlockSpec((rows, tn), lambda j: (0, j)),
        out_shape=jax.ShapeDtypeStruct((rows, n_out), F32),
        compiler_params=_cparams(("arbitrary",), 40),
        name="ada_mod",
    )(c_all, w_ada, b_ada.reshape(1, n_out))


def _rope_kernel(c_ref, s1_ref, s2_ref, *, pos0, stride):
    n = c_ref.shape[0]
    lane = lax.broadcasted_iota(I32, (n, LANES), 1)
    row = lax.broadcasted_iota(I32, (n, LANES), 0) + pl.program_id(0) * n
    pos = (row * stride + pos0).astype(F32)
    half = ROT_DIM // 2
    fi = (lane & (half - 1)).astype(F32)
    inv = jnp.exp(-math.log(ROPE_THETA) * fi / half)
    ang = pos * inv
    cos = jnp.cos(ang)
    sin = jnp.sin(ang)
    c_ref[...] = jnp.where(lane < ROT_DIM, cos, 1.0)
    s1_ref[...] = jnp.where(lane < half, -sin, 0.0)
    s2_ref[...] = jnp.where((lane >= half) & (lane < ROT_DIM), sin, 0.0)


def _rope_tables(n_rows, pos0, stride):
    tr = min(n_rows, 512)
    spec = pl.BlockSpec((tr, LANES), lambda i: (i, 0))
    shp = jax.ShapeDtypeStruct((n_rows, LANES), F32)
    return pl.pallas_call(
        functools.partial(_rope_kernel, pos0=pos0, stride=stride),
        grid=(n_rows // tr,),
        in_specs=[],
        out_specs=[spec, spec, spec],
        out_shape=[shp, shp, shp],
        compiler_params=_cparams(("arbitrary",), 32),
        name="rope_tables",
    )()


def _inproj_kernel(x_ref, sc_ref, sh_ref, ng_ref, w_ref, qkg_ref, rc_ref, rs1_ref, rs2_ref,
                   o_ref, h_scr):
    n = pl.program_id(1)

    @pl.when(n == 0)
    def _():
        h = _rms(x_ref[...], ng_ref[...]) * (1.0 + sc_ref[...]) + sh_ref[...]
        h_scr[...] = h.astype(BF16)

    o_ref[...] = _dot(h_scr[...], w_ref[...])

    @pl.when(n < 2)
    def _():
        g = qkg_ref[n]
        c = rc_ref[...]
        s1 = rs1_ref[...]
        s2 = rs2_ref[...]
        for hh in range(N_HEADS):
            sl = slice(hh * HEAD_DIM, (hh + 1) * HEAD_DIM)
            y = _rms(o_ref[:, sl], g)
            o_ref[:, sl] = (y * c + pltpu.roll(y, HEAD_DIM - ROT_DIM // 2, 1) * s1
                            + pltpu.roll(y, ROT_DIM // 2, 1) * s2)


def _in_proj(x, mod, ng, w_bf, qkg, rope, *, tm, rows_per_mod, rope_blocks):
    t = x.shape[0]
    tn = 1024
    rm = mod.shape[1]
    mt = t // tm

    def mod_map(j):
        return lambda m, n: ((m * tm) // rows_per_mod, 0, j)

    rope_spec = pl.BlockSpec((tm, LANES), lambda m, n: (m % rope_blocks, 0))
    return pl.pallas_call(
        _inproj_kernel,
        grid=(mt, D_IN_PROJ // tn),
        in_specs=[pl.BlockSpec((tm, D_MODEL), lambda m, n: (m, 0)),
                  pl.BlockSpec((None, rm, D_MODEL), mod_map(1)),
                  pl.BlockSpec((None, rm, D_MODEL), mod_map(0)),
                  pl.BlockSpec((1, D_MODEL), lambda m, n: (0, 0)),
                  pl.BlockSpec((D_MODEL, tn), lambda m, n: (0, n)),
                  pl.BlockSpec((2, 1, HEAD_DIM), lambda m, n: (0, 0, 0)),
                  rope_spec, rope_spec, rope_spec],
        out_specs=pl.BlockSpec((tm, tn), lambda m, n: (m, n)),
        out_shape=jax.ShapeDtypeStruct((t, D_IN_PROJ), F32),
        scratch_shapes=[pltpu.VMEM((tm, D_MODEL), BF16)],
        compiler_params=_cparams(("arbitrary", "arbitrary"), 48),
        name="in_proj",
    )(x, mod, mod, ng, w_bf, qkg, *rope)


ATTN_GROUP = 2


def _attn_prompt_kernel(q_ref, k_ref, v_ref, o_ref, o0, o1, o2, l0, l1, l2):
    scale = 1.0 / math.sqrt(HEAD_DIM)
    qi = lax.broadcasted_iota(I32, (SPAN, 2 * SPAN), 0)
    kj = lax.broadcasted_iota(I32, (SPAN, 2 * SPAN), 1)
    is_prev = kj < SPAN
    prev_mask = is_prev & (kj >= qi)
    own_mask = (~is_prev) & (kj - SPAN <= qi)
    neg = -jnp.inf
    oaccs = (o0, o1, o2)
    laccs = (l0, l1, l2)

    for p, d in enumerate(DILATIONS):
        nbk = SEQ // (SPAN * d)
        shift = nbk.bit_length() - 1
        oacc = oaccs[p]
        lacc = laccs[p]

        def rows(start, d=d):
            return pl.ds(start, SPAN) if d == 1 else pl.ds(start, SPAN, stride=d)

        def body(it, carry, d=d, nbk=nbk, shift=shift, oacc=oacc, lacc=lacc, rows=rows):
            loaded = []
            for j in range(ATTN_GROUP):
                idx = it * ATTN_GROUP + j
                r = idx >> shift
                n = idx & (nbk - 1)
                start = r + n * (SPAN * d)
                pstart = jnp.maximum(start - SPAN * d, r)
                q = q_ref[rows(start), :]
                kk = jnp.concatenate([k_ref[rows(pstart), :], k_ref[rows(start), :]], axis=0)
                vv = jnp.concatenate([v_ref[rows(pstart), :], v_ref[rows(start), :]], axis=0)
                loaded.append((start, n, q, kk, vv))
            results = []
            for start, n, q, kk, vv in loaded:
                valid = own_mask | (prev_mask & (n > 0))
                s = jnp.where(valid, _dot_nt(q.astype(BF16), kk.astype(BF16)) * scale, neg)
                m = jnp.max(s, axis=-1, keepdims=True)
                pr = jnp.exp(s - m)
                den = jnp.sum(pr, axis=-1, keepdims=True)
                o = _dot(pr.astype(BF16), vv.astype(BF16)) / den
                results.append((start, o, m + jnp.log(den)))
            for start, o, lse in results:
                oacc[rows(start), :] = o
                lacc[rows(start), :] = jnp.broadcast_to(lse, (SPAN, HEAD_DIM))
            return carry

        lax.fori_loop(0, SEQ // SPAN // ATTN_GROUP, body, 0)

    ch = 512
    for c0 in range(0, SEQ, ch):
        sl = slice(c0, c0 + ch)
        la, lb, lc = l0[sl, :], l1[sl, :], l2[sl, :]
        mx = jnp.maximum(jnp.maximum(la, lb), lc)
        wa, wb, wc = jnp.exp(la - mx), jnp.exp(lb - mx), jnp.exp(lc - mx)
        o_ref[sl, :] = (wa * o0[sl, :] + wb * o1[sl, :] + wc * o2[sl, :]) / (wa + wb + wc)


def _attn_prompt(proj, batch):
    blk = (SEQ, HEAD_DIM)
    scr = pltpu.VMEM((SEQ, HEAD_DIM), F32)
    return pl.pallas_call(
        _attn_prompt_kernel,
        grid=(batch, N_HEADS),
        in_specs=[pl.BlockSpec(blk, lambda b, h: (b, h)),
                  pl.BlockSpec(blk, lambda b, h: (b, N_HEADS + h)),
                  pl.BlockSpec(blk, lambda b, h: (b, 2 * N_HEADS + h))],
        out_specs=pl.BlockSpec(blk, lambda b, h: (b, h)),
        out_shape=jax.ShapeDtypeStruct((batch * SEQ, D_ATTN), F32),
        scratch_shapes=[scr] * 6,
        compiler_params=_cparams(("arbitrary", "arbitrary"), 48),
        name="attn_prompt",
    )(proj, proj, proj)


def _attn_sample_kernel(q_ref, kn_ref, vn_ref, k1, k4, k16, v1, v4, v16, o_ref):
    scale = 1.0 / math.sqrt(HEAD_DIM)
    q = q_ref[...]
    kn = kn_ref[...]
    vn = vn_ref[...]
    s_new = jnp.sum(q * kn, axis=-1, keepdims=True) * scale
    outs = []
    lses = []
    for kc, vc in ((k1, v1), (k4, v4), (k16, v16)):
        kk = kc[...]
        vv = vc[...]
        s = jnp.sum(kk * q[None], axis=-1, keepdims=True) * scale
        m = jnp.maximum(jnp.max(s, axis=0), s_new)
        p = jnp.exp(s - m[None])
        p_new = jnp.exp(s_new - m)
        den = jnp.sum(p, axis=0) + p_new
        o = (jnp.sum(p * vv, axis=0) + p_new * vn) / den
        outs.append(o)
        lses.append(m + jnp.log(den))
    mx = jnp.maximum(jnp.maximum(lses[0], lses[1]), lses[2])
    ws = [jnp.exp(l - mx) for l in lses]
    o_ref[...] = (ws[0] * outs[0] + ws[1] * outs[1] + ws[2] * outs[2]) / (ws[0] + ws[1] + ws[2])


def _attn_sample(q, kn, vn, cache_k, cache_v):
    b = q.shape[0]
    row = pl.BlockSpec((None, N_HEADS, HEAD_DIM), lambda i: (i, 0, 0))
    views, specs = [], []
    for c in (cache_k, cache_v):
        for d in DILATIONS:
            views.append(c.reshape(b, WBUF // d, d, N_HEADS, HEAD_DIM))
            last = WBUF // d // SPAN - 1
            specs.append(pl.BlockSpec((None, SPAN, None, N_HEADS, HEAD_DIM),
                                      lambda i, last=last: (i, last, 0, 0, 0)))
    return pl.pallas_call(
        _attn_sample_kernel,
        grid=(b,),
        in_specs=[row, row, row] + specs,
        out_specs=row,
        out_shape=jax.ShapeDtypeStruct((b, N_HEADS, HEAD_DIM), F32),
        compiler_params=_cparams(("arbitrary",), 32),
        name="attn_sample",
    )(q, kn, vn, *views)


CACHE_CHUNK = 256


def _cache_kernel(c_ref, n_ref, o_ref):
    for r0 in range(0, WBUF - 1, CACHE_CHUNK):
        r1 = min(r0 + CACHE_CHUNK, WBUF - 1)
        o_ref[r0:r1] = c_ref[r0 + 1:r1 + 1]
    o_ref[WBUF - 1] = n_ref[...]


def _cache_update(cache, new):
    b = cache.shape[0]
    blk = pl.BlockSpec((None, WBUF, N_HEADS, HEAD_DIM), lambda i: (i, 0, 0, 0))
    return pl.pallas_call(
        _cache_kernel,
        grid=(b,),
        in_specs=[blk, pl.BlockSpec((None, N_HEADS, HEAD_DIM), lambda i: (i, 0, 0))],
        out_specs=blk,
        out_shape=jax.ShapeDtypeStruct(cache.shape, cache.dtype),
        compiler_params=_cparams(("arbitrary",), 40),
        name="cache_update",
    )(cache, new)


def _ssm_prep_kernel(ar_ref, ai_ref, ldt_ref, br_ref, bi_ref, cr_ref, ci_ref,
                     ab_ref, wb_ref, wc_ref):
    ar = ar_ref[...]
    ai = ai_ref[...]
    dt = jnp.exp(ldt_ref[...])
    mag = jnp.exp(dt * ar)
    abar_re = mag * jnp.cos(dt * ai)
    abar_im = mag * jnp.sin(dt * ai)
    nr = abar_re - 1.0
    den = ar * ar + ai * ai
    coef_re = (nr * ar + abar_im * ai) / den
    coef_im = (abar_im * ar - nr * ai) / den
    ab_ref[0] = abar_re
    ab_ref[1] = abar_im
    wb_ref[...] = jnp.zeros(wb_ref.shape, wb_ref.dtype)
    wc_ref[...] = jnp.zeros(wc_ref.shape, wc_ref.dtype)
    lane = lax.broadcasted_iota(I32, (SSM_CH, LANES), 1)
    gpt = 16
    for g in range(SSM_GROUPS):
        kt, gl = divmod(g, gpt)
        keep = (lane < SSM_STATE) if g % 2 == 0 else (lane >= SSM_STATE)
        cre = coef_re[g:g + 1, :]
        cim = coef_im[g:g + 1, :]
        bre = br_ref[g]
        bim = bi_ref[g]
        bb_re = jnp.where(keep, cre * bre - cim * bim, 0.0).astype(BF16)
        bb_im = jnp.where(keep, cre * bim + cim * bre, 0.0).astype(BF16)
        rs = slice(gl * SSM_CH, (gl + 1) * SSM_CH)
        lt = (gl // 2) * LANES
        wb_ref[kt, rs, lt:lt + LANES] = bb_re
        wb_ref[kt, rs, 1024 + lt:1024 + lt + LANES] = bb_im
        c_re = jnp.where(keep, cr_ref[g], 0.0).astype(BF16)
        c_im = jnp.where(keep, -ci_ref[g], 0.0).astype(BF16)
        wc_ref[kt, rs, lt:lt + LANES] = c_re
        wc_ref[kt, rs, 1024 + lt:1024 + lt + LANES] = c_im


def _ssm_prep(a_re, a_im, log_dt, b_re, b_im, c_re, c_im):
    dup = lambda x: jnp.concatenate([x, x], axis=-1)
    br_t = dup(jnp.swapaxes(b_re, 1, 2))
    bi_t = dup(jnp.swapaxes(b_im, 1, 2))
    shapes = [jax.ShapeDtypeStruct((2, SSM_GROUPS, LANES), F32),
              jax.ShapeDtypeStruct((4, 256, 2048), BF16),
              jax.ShapeDtypeStruct((4, 256, 2048), BF16)]
    ab, wb, wct = pl.pallas_call(
        _ssm_prep_kernel,
        out_shape=shapes,
        compiler_params=pltpu.CompilerParams(vmem_limit_bytes=40 * MIB),
        name="ssm_prep",
    )(dup(a_re), dup(a_im), log_dt.reshape(SSM_GROUPS, 1), br_t, bi_t, dup(c_re), dup(c_im))
    abar = ab[:, :, :SSM_STATE].reshape(2, N_STATE)
    return abar, wb, wct


def _ssm_bu(ub, wb_ref):
    return [_dot(ub[:, kt * 256:(kt + 1) * 256], wb_ref[kt]) for kt in range(4)]


def _ssm_tail(h_re_tiles, h_im_tiles, u, wc_ref, d_ref, wg_ref, bg_ref, og_ref):
    ys = []
    for kt in range(4):
        wct = wc_ref[kt]
        ys.append(_dot_nt(h_re_tiles[kt].astype(BF16), wct[:, :1024])
                  + _dot_nt(h_im_tiles[kt].astype(BF16), wct[:, 1024:]))
    y = jnp.concatenate(ys, axis=-1) + d_ref[...] * u
    g = jax.nn.gelu(y)
    z = _dot(g.astype(BF16), wg_ref[...]) + bg_ref[...]
    out = g * jax.nn.sigmoid(z)
    return _rms(out, og_ref[...])


SCAN_LANES = 256


def _ssm_prompt_kernel(u_ref, wb_ref, wc_ref, ab_ref, d_ref, wg_ref, bg_ref, og_ref,
                       o_ref, hre_ref, him_ref, bre, bim, cre, cim):
    c = pl.program_id(1)
    tc = u_ref.shape[0]

    @pl.when(c == 0)
    def _():
        cre[...] = jnp.zeros(cre.shape, F32)
        cim[...] = jnp.zeros(cim.shape, F32)

    u = u_ref[...]
    bu = _ssm_bu(u.astype(BF16), wb_ref)
    for kt in range(4):
        bre[:, kt * 1024:(kt + 1) * 1024] = bu[kt][:, :1024]
        bim[:, kt * 1024:(kt + 1) * 1024] = bu[kt][:, 1024:]

    row = lax.broadcasted_iota(I32, (SUBLANES, SCAN_LANES), 0)

    def cmul(xr, xi, yr, yi):
        return xr * yr - xi * yi, xr * yi + xi * yr

    for lc in range(N_STATE // SCAN_LANES):
        sl = slice(lc * SCAN_LANES, (lc + 1) * SCAN_LANES)
        a1r = jnp.broadcast_to(ab_ref[0:1, sl], (SUBLANES, SCAN_LANES))
        a1i = jnp.broadcast_to(ab_ref[1:2, sl], (SUBLANES, SCAN_LANES))
        a2r, a2i = cmul(a1r, a1i, a1r, a1i)
        a4r, a4i = cmul(a2r, a2i, a2r, a2i)
        a8r, a8i = cmul(a4r, a4i, a4r, a4i)
        pwr = jnp.ones((SUBLANES, SCAN_LANES), F32)
        pwi = jnp.zeros((SUBLANES, SCAN_LANES), F32)
        for bit, (fr, fi) in ((1, (a1r, a1i)), (2, (a2r, a2i)), (4, (a4r, a4i)), (8, (a8r, a8i))):
            has = ((row + 1) & bit) != 0
            nr, ni = cmul(pwr, pwi, fr, fi)
            pwr = jnp.where(has, nr, pwr)
            pwi = jnp.where(has, ni, pwi)
        steps = ((1, a1r, a1i), (2, a2r, a2i), (4, a4r, a4i))

        def body(m, carry, sl=sl, steps=steps, pwr=pwr, pwi=pwi):
            cr, ci = carry
            off = pl.multiple_of(m * SUBLANES, SUBLANES)
            xr = bre[pl.ds(off, SUBLANES), sl]
            xi = bim[pl.ds(off, SUBLANES), sl]
            for s, fr, fi in steps:
                sr = jnp.where(row >= s, pltpu.roll(xr, s, 0), 0.0)
                si = jnp.where(row >= s, pltpu.roll(xi, s, 0), 0.0)
                tr, ti = cmul(fr, fi, sr, si)
                xr = xr + tr
                xi = xi + ti
            tr, ti = cmul(pwr, pwi, cr, ci)
            xr = xr + tr
            xi = xi + ti
            bre[pl.ds(off, SUBLANES), sl] = xr
            bim[pl.ds(off, SUBLANES), sl] = xi
            cr = jnp.broadcast_to(xr[SUBLANES - 1:SUBLANES, :], (SUBLANES, SCAN_LANES))
            ci = jnp.broadcast_to(xi[SUBLANES - 1:SUBLANES, :], (SUBLANES, SCAN_LANES))
            return cr, ci

        cr, ci = lax.fori_loop(0, tc // SUBLANES, body, (cre[:, sl], cim[:, sl]), unroll=2)
        cre[:, sl] = cr
        cim[:, sl] = ci

    h_re = [bre[:, kt * 1024:(kt + 1) * 1024] for kt in range(4)]
    h_im = [bim[:, kt * 1024:(kt + 1) * 1024] for kt in range(4)]
    o_ref[...] = _ssm_tail(h_re, h_im, u, wc_ref, d_ref, wg_ref, bg_ref, og_ref).astype(BF16)

    @pl.when(c == pl.num_programs(1) - 1)
    def _():
        hre_ref[...] = cre[0:1, :]
        him_ref[...] = cim[0:1, :]


def _ssm_prompt(proj, batch, abar, wb, wct, ssm_d, wglu_bf, bglu, og):
    tc = 256
    nchunk = SEQ // tc
    const2 = lambda shape: pl.BlockSpec(shape, lambda b, c: (0,) * len(shape))
    st_spec = pl.BlockSpec((None, 1, N_STATE), lambda b, c: (b, 0, 0))
    st_shape = jax.ShapeDtypeStruct((batch, 1, N_STATE), F32)
    return pl.pallas_call(
        _ssm_prompt_kernel,
        grid=(batch, nchunk),
        in_specs=[pl.BlockSpec((tc, D_SSM), lambda b, c: (b * nchunk + c, 3)),
                  const2((4, 256, 2048)), const2((4, 256, 2048)), const2((2, N_STATE)),
                  const2((1, D_SSM)), const2((D_SSM, D_SSM)), const2((1, D_SSM)), const2((1, D_SSM))],
        out_specs=[pl.BlockSpec((tc, D_SSM), lambda b, c: (b * nchunk + c, 0)), st_spec, st_spec],
        out_shape=[jax.ShapeDtypeStruct((batch * SEQ, D_SSM), BF16), st_shape, st_shape],
        scratch_shapes=[pltpu.VMEM((tc, N_STATE), F32), pltpu.VMEM((tc, N_STATE), F32),
                        pltpu.VMEM((SUBLANES, N_STATE), F32), pltpu.VMEM((SUBLANES, N_STATE), F32)],
        compiler_params=_cparams(("arbitrary", "arbitrary"), 48),
        name="ssm_prompt",
    )(proj, wb, wct, abar, ssm_d, wglu_bf, bglu, og)


def _ssm_sample_kernel(u_ref, h0r_ref, h0i_ref, wb_ref, wc_ref, ab_ref, d_ref, wg_ref, bg_ref, og_ref,
                       o_ref, hre_ref, him_ref):
    u = u_ref[...]
    bu = _ssm_bu(u.astype(BF16), wb_ref)
    h_re, h_im = [], []
    for kt in range(4):
        sl = slice(kt * 1024, (kt + 1) * 1024)
        ar = ab_ref[0:1, sl]
        ai = ab_ref[1:2, sl]
        h0r = h0r_ref[:, sl]
        h0i = h0i_ref[:, sl]
        hr = bu[kt][:, :1024] + (ar * h0r - ai * h0i)
        hi = bu[kt][:, 1024:] + (ar * h0i + ai * h0r)
        hre_ref[:, sl] = hr
        him_ref[:, sl] = hi
        h_re.append(hr)
        h_im.append(hi)
    o_ref[...] = _ssm_tail(h_re, h_im, u, wc_ref, d_ref, wg_ref, bg_ref, og_ref).astype(BF16)


def _ssm_sample(u, h0_re, h0_im, abar, wb, wct, ssm_d, wglu_bf, bglu, og):
    b = u.shape[0]
    st_shape = jax.ShapeDtypeStruct((b, N_STATE), F32)
    return pl.pallas_call(
        _ssm_sample_kernel,
        out_shape=[jax.ShapeDtypeStruct((b, D_SSM), BF16), st_shape, st_shape],
        compiler_params=pltpu.CompilerParams(vmem_limit_bytes=40 * MIB),
        name="ssm_sample",
    )(u, h0_re, h0_im, wb, wct, abar, ssm_d, wglu_bf, bglu, og)


def _outproj_kernel(attn_ref, ssm_ref, ag_ref, w_ref, x_ref, g1_ref, o_ref, lhs):
    n = pl.program_id(1)

    @pl.when(n == 0)
    def _():
        lhs[:, :D_ATTN] = _rms(attn_ref[...], ag_ref[...]).astype(BF16)
        lhs[:, D_ATTN:] = ssm_ref[...]

    o_ref[...] = x_ref[...] + g1_ref[...] * _dot(lhs[...], w_ref[...])


def _out_proj(attn, ssm_n, ag, w_bf, x, mod, *, tm, rows_per_mod):
    t = x.shape[0]
    tn = 1024
    rm = mod.shape[1]
    return pl.pallas_call(
        _outproj_kernel,
        grid=(t // tm, D_MODEL // tn),
        in_specs=[pl.BlockSpec((tm, D_ATTN), lambda m, n: (m, 0)),
                  pl.BlockSpec((tm, D_SSM), lambda m, n: (m, 0)),
                  pl.BlockSpec((1, D_ATTN), lambda m, n: (0, 0)),
                  pl.BlockSpec((D_MODEL, tn), lambda m, n: (0, n)),
                  pl.BlockSpec((tm, tn), lambda m, n: (m, n)),
                  pl.BlockSpec((None, rm, tn), lambda m, n: ((m * tm) // rows_per_mod, 0, 2 * (D_MODEL // tn) + n))],
        out_specs=pl.BlockSpec((tm, tn), lambda m, n: (m, n)),
        out_shape=jax.ShapeDtypeStruct((t, D_MODEL), F32),
        scratch_shapes=[pltpu.VMEM((tm, D_MODEL), BF16)],
        compiler_params=_cparams(("arbitrary", "arbitrary"), 48),
        name="out_proj",
    )(attn, ssm_n, ag, w_bf, x, mod)


def _route_kernel(x_ref, sc_ref, sh_ref, ng_ref, wr_ref, br_ref, h2_ref, te_ref, gt_ref):
    tm = x_ref.shape[0]
    h2 = _rms(x_ref[...], ng_ref[...]) * (1.0 + sc_ref[...]) + sh_ref[...]
    h2_ref[...] = h2
    scores = jax.nn.sigmoid(_dot_nt(wr_ref[...], h2.astype(BF16)))
    biased = scores + br_ref[...]
    neg = -jnp.inf
    per_g = N_EXPERTS // N_EXPERT_GROUPS
    gi_iota = lax.broadcasted_iota(I32, (per_g, tm), 0)
    gscore = []
    for g in range(N_EXPERT_GROUPS):
        xg = biased[g * per_g:(g + 1) * per_g, :]
        m1 = jnp.max(xg, axis=0, keepdims=True)
        i1 = jnp.min(jnp.where(xg == m1, gi_iota, per_g), axis=0, keepdims=True)
        m2 = jnp.max(jnp.where(gi_iota == i1, neg, xg), axis=0, keepdims=True)
        gscore.append(m1 + m2)
    parts = []
    for i in range(N_EXPERT_GROUPS):
        rank = jnp.zeros((1, tm), I32)
        for j in range(N_EXPERT_GROUPS):
            if j == i:
                continue
            ahead = (gscore[j] >= gscore[i]) if j < i else (gscore[j] > gscore[i])
            rank = rank + ahead.astype(I32)
        parts.append(jnp.where(rank < TOPK_GROUPS, biased[i * per_g:(i + 1) * per_g, :], neg))
    cur = jnp.concatenate(parts, axis=0)
    e_iota = lax.broadcasted_iota(I32, (N_EXPERTS, tm), 0)
    idxs, gates = [], []
    for _ in range(TOP_K):
        m = jnp.max(cur, axis=0, keepdims=True)
        idx = jnp.min(jnp.where(cur == m, e_iota, N_EXPERTS), axis=0, keepdims=True)
        hit = e_iota == idx
        gates.append(jnp.sum(jnp.where(hit, scores, 0.0), axis=0, keepdims=True))
        idxs.append(idx)
        cur = jnp.where(hit, neg, cur)
    tot = gates[0]
    for g in gates[1:]:
        tot = tot + g
    for k in range(TOP_K):
        te_ref[k:k + 1, :] = idxs[k]
        gt_ref[k:k + 1, :] = gates[k] / tot * ROUTE_SCALE


def _route(x1, mod, ng, wr_t_bf, b_router, *, tm, rows_per_mod):
    t = x1.shape[0]
    rm = mod.shape[1]

    def mod_map(j):
        return lambda m: ((m * tm) // rows_per_mod, 0, j)

    return pl.pallas_call(
        _route_kernel,
        grid=(t // tm,),
        in_specs=[pl.BlockSpec((tm, D_MODEL), lambda m: (m, 0)),
                  pl.BlockSpec((None, rm, D_MODEL), mod_map(4)),
                  pl.BlockSpec((None, rm, D_MODEL), mod_map(3)),
                  pl.BlockSpec((1, D_MODEL), lambda m: (0, 0)),
                  pl.BlockSpec((N_EXPERTS, D_MODEL), lambda m: (0, 0)),
                  pl.BlockSpec((N_EXPERTS, 1), lambda m: (0, 0))],
        out_specs=[pl.BlockSpec((tm, D_MODEL), lambda m: (m, 0)),
                   pl.BlockSpec((TOP_K, tm), lambda m: (0, m)),
                   pl.BlockSpec((TOP_K, tm), lambda m: (0, m))],
        out_shape=[jax.ShapeDtypeStruct((t, D_MODEL), F32),
                   jax.ShapeDtypeStruct((TOP_K, t), I32),
                   jax.ShapeDtypeStruct((TOP_K, t), F32)],
        compiler_params=_cparams(("arbitrary",), 40),
        name="route",
    )(x1, mod, mod, ng, wr_t_bf, b_router.reshape(N_EXPERTS, 1))


DISP_TM = 256


def _rank_kernel(te_ref, rank_ref, cnt_ref, run):
    i = pl.program_id(0)
    tm = te_ref.shape[1]

    @pl.when(i == 0)
    def _():
        run[...] = jnp.zeros(run.shape, F32)

    te = te_ref[...]
    e_iota = lax.broadcasted_iota(I32, (N_EXPERTS, tm), 0)
    a = jnp.zeros((N_EXPERTS, tm), F32)
    for k in range(TOP_K):
        a = a + (e_iota == te[k:k + 1, :]).astype(F32)
    upper = (lax.broadcasted_iota(I32, (tm, tm), 0) < lax.broadcasted_iota(I32, (tm, tm), 1)).astype(BF16)
    tot = _dot(a.astype(BF16), upper) + run[:, 0:1]
    for k in range(TOP_K):
        rk = jnp.sum(jnp.where(e_iota == te[k:k + 1, :], tot, 0.0), axis=0, keepdims=True)
        rank_ref[k:k + 1, :] = rk.astype(I32)
    run[...] = run[...] + jnp.sum(a, axis=1, keepdims=True)

    @pl.when(i == pl.num_programs(0) - 1)
    def _():
        cnt_ref[...] = run[...].astype(I32)


def _dest_kernel(te_ref, rank_ref, cnt_ref, dest_ref, ps_ref, nblk_ref, ps_scr):
    i = pl.program_id(0)
    tm = te_ref.shape[1]

    @pl.when(i == 0)
    def _():
        nb_e = ((cnt_ref[...] + (MOE_BLK - 1)) >> MOE_SHIFT).astype(F32)
        lower = (lax.broadcasted_iota(I32, (N_EXPERTS, N_EXPERTS), 1)
                 < lax.broadcasted_iota(I32, (N_EXPERTS, N_EXPERTS), 0)).astype(BF16)
        ps_b = _dot(lower, nb_e.astype(BF16))
        ps_scr[...] = ps_b * MOE_BLK
        ps_ref[...] = (ps_b * MOE_BLK).astype(I32)
        nblk_ref[...] = jnp.max(ps_b + nb_e, axis=0, keepdims=True).astype(I32)

    te = te_ref[...]
    rk = rank_ref[...]
    ps = ps_scr[:, 0:1]
    e_iota = lax.broadcasted_iota(I32, (N_EXPERTS, tm), 0)
    for k in range(TOP_K):
        tek = te[k:k + 1, :]
        base = jnp.sum(jnp.where(e_iota == tek, ps, 0.0), axis=0, keepdims=True)
        dest_ref[k:k + 1, :] = jnp.where(tek >= 0, base.astype(I32) + rk[k:k + 1, :], -1)


def _dispatch_tables(te):
    tpad = te.shape[1]
    tm = DISP_TM
    tile = pl.BlockSpec((TOP_K, tm), lambda i: (0, i))
    full = lambda shape: pl.BlockSpec(shape, lambda i: (0,) * len(shape))
    rank, cnt = pl.pallas_call(
        _rank_kernel,
        grid=(tpad // tm,),
        in_specs=[tile],
        out_specs=[tile, full((N_EXPERTS, LANES))],
        out_shape=[jax.ShapeDtypeStruct((TOP_K, tpad), I32),
                   jax.ShapeDtypeStruct((N_EXPERTS, LANES), I32)],
        scratch_shapes=[pltpu.VMEM((N_EXPERTS, LANES), F32)],
        compiler_params=_cparams(("arbitrary",), 32),
        name="moe_rank",
    )(te)
    dest, ps, nblk = pl.pallas_call(
        _dest_kernel,
        grid=(tpad // tm,),
        in_specs=[tile, tile, full((N_EXPERTS, LANES))],
        out_specs=[tile, full((N_EXPERTS, LANES)), full((1, LANES))],
        out_shape=[jax.ShapeDtypeStruct((TOP_K, tpad), I32),
                   jax.ShapeDtypeStruct((N_EXPERTS, LANES), I32),
                   jax.ShapeDtypeStruct((1, LANES), I32)],
        scratch_shapes=[pltpu.VMEM((N_EXPERTS, LANES), F32)],
        compiler_params=_cparams(("arbitrary",), 32),
        name="moe_dest",
    )(te, rank, cnt)
    return dest, nblk[0, :1], cnt[:, 0], ps[:, 0]


PAD_BITS = (64, 32, 16, 8)


def _scatter_kernel(cnt_ref, ps_ref, nblk_ref, dest_ref, hp_ref, hs_ref, xs_ref, zeros, sem, zsem, *,
                    n_prompt_tiles):
    i = pl.program_id(0)
    tm = dest_ref.shape[1]

    def row_copy(src_ref, r, d):
        return pltpu.make_async_copy(src_ref.at[pl.ds(r, 1)], xs_ref.at[pl.ds(d, 1)], sem)

    def scatter_rows(src_ref, n_rows):
        def issue(r, c):
            for k in range(TOP_K):
                row_copy(src_ref, r, dest_ref[k, r]).start()
            return c

        def drain(r, c):
            for k in range(TOP_K):
                row_copy(src_ref, r, dest_ref[k, r]).wait()
            return c

        lax.fori_loop(0, n_rows, issue, 0, unroll=2)
        lax.fori_loop(0, n_rows, drain, 0, unroll=2)

    @pl.when(i < n_prompt_tiles)
    def _():
        scatter_rows(hp_ref, tm)

    @pl.when(i == n_prompt_tiles)
    def _():
        scatter_rows(hs_ref, hs_ref.shape[0])

    @pl.when(i == 0)
    def _():
        zeros[...] = jnp.zeros(zeros.shape, F32)

        def pad_copies(e, do):
            n = cnt_ref[e]
            npad = (((n + (MOE_BLK - 1)) >> MOE_SHIFT) << MOE_SHIFT) - n
            base = ps_ref[e] + n
            head = npad & (SUBLANES - 1)
            for s in range(SUBLANES - 1):
                @pl.when(s < head)
                def _(s=s):
                    do(pltpu.make_async_copy(zeros.at[pl.ds(0, 1)], xs_ref.at[pl.ds(base + s, 1)], zsem))
            off = base + head
            for bit in PAD_BITS:
                take = (npad & bit) != 0

                @pl.when(take)
                def _(off=off, bit=bit):
                    dst = xs_ref.at[pl.ds(pl.multiple_of(off, SUBLANES), bit)]
                    do(pltpu.make_async_copy(zeros.at[pl.ds(0, bit)], dst, zsem))
                off = off + jnp.where(take, bit, 0)

        def issue(e, c):
            pad_copies(e, lambda cp: cp.start())
            return c

        def drain(e, c):
            pad_copies(e, lambda cp: cp.wait())
            return c

        lax.fori_loop(0, N_EXPERTS, issue, 0)
        lax.fori_loop(0, N_EXPERTS, drain, 0)

        def tail_copy(b):
            dst = xs_ref.at[pl.ds(pl.multiple_of(b * MOE_BLK, MOE_BLK), MOE_BLK)]
            return pltpu.make_async_copy(zeros, dst, zsem)

        def tail_issue(b, c):
            @pl.when(b >= nblk_ref[0])
            def _():
                tail_copy(b).start()
            return c

        def tail_drain(b, c):
            @pl.when(b >= nblk_ref[0])
            def _():
                tail_copy(b).wait()
            return c

        n_blocks = xs_ref.shape[0] // MOE_BLK
        lax.fori_loop(0, n_blocks, tail_issue, 0)
        lax.fori_loop(0, n_blocks, tail_drain, 0)


def _scatter_tokens(dest, cnt, ps, nblk, h2_p, h2_s, n_slots):
    tm = DISP_TM
    n_prompt_tiles = h2_p.shape[0] // tm
    grid_spec = pltpu.PrefetchScalarGridSpec(
        num_scalar_prefetch=3,
        grid=(n_prompt_tiles + 1,),
        in_specs=[pl.BlockSpec((TOP_K, tm), lambda i, *_: (0, i), memory_space=pltpu.SMEM),
                  pl.BlockSpec((tm, D_MODEL), lambda i, *_: (jnp.minimum(i, n_prompt_tiles - 1), 0)),
                  pl.BlockSpec(h2_s.shape, lambda i, *_: (0, 0))],
        out_specs=pl.BlockSpec(memory_space=pl.ANY),
        scratch_shapes=[pltpu.VMEM((MOE_BLK, D_MODEL), F32),
                        pltpu.SemaphoreType.DMA(()), pltpu.SemaphoreType.DMA(())],
    )
    return pl.pallas_call(
        functools.partial(_scatter_kernel, n_prompt_tiles=n_prompt_tiles),
        grid_spec=grid_spec,
        out_shape=jax.ShapeDtypeStruct((n_slots, D_MODEL), F32),
        compiler_params=_cparams(("arbitrary",), 32),
        name="moe_scatter",
    )(cnt, ps, nblk, dest, h2_p, h2_s)


ROW_DMA_PRIORITY = 1

def _moe_kernel(cnt_ref, ps_ref, nblk_ref, wg_ref, wu_ref, wd_ref, xs_ref, ys_ref,
                wgb, wub, wdb, xbuf, ybuf, xsem, ysem):
    e = pl.program_id(0)
    total = nblk_ref[0]
    nb = (cnt_ref[e] + (MOE_BLK - 1)) >> MOE_SHIFT
    g0 = ps_ref[e] >> MOE_SHIFT

    def rows(g):
        return pl.ds(pl.multiple_of(g * MOE_BLK, MOE_BLK), MOE_BLK)

    def x_copy(g, slot):
        return pltpu.make_async_copy(xs_ref.at[rows(g)], xbuf.at[slot], xsem.at[slot])

    def y_copy(g, slot):
        return pltpu.make_async_copy(ybuf.at[slot], ys_ref.at[rows(g)], ysem.at[slot])

    @pl.when(e == 0)
    def _():
        x_copy(0, 0).start(priority=ROW_DMA_PRIORITY)

    @pl.when(nb > 0)
    def _():
        wgb[...] = wg_ref[...].astype(BF16)
        wub[...] = wu_ref[...].astype(BF16)
        wdb[...] = wd_ref[...].astype(BF16)

        def body(c, carry):
            g = g0 + c
            slot = g & 1

            @pl.when(g + 1 < total)
            def _():
                x_copy(g + 1, 1 - slot).start(priority=ROW_DMA_PRIORITY)

            x_copy(g, slot).wait()

            @pl.when(g >= 2)
            def _():
                y_copy(g - 2, slot).wait()

            x = xbuf[slot].astype(BF16)
            hid = jax.nn.silu(_dot(x, wgb[...])) * _dot(x, wub[...])
            ybuf[slot] = _dot(hid.astype(BF16), wdb[...])
            y_copy(g, slot).start(priority=ROW_DMA_PRIORITY)
            return carry

        lax.fori_loop(0, nb, body, 0)

    @pl.when(e == pl.num_programs(0) - 1)
    def _():
        @pl.when(total >= 2)
        def _():
            y_copy(total - 2, total & 1).wait()

        y_copy(total - 1, (total - 1) & 1).wait()

        ybuf[0] = jnp.zeros((MOE_BLK, D_MODEL), F32)

        def tail_issue(b, c):
            @pl.when(b >= total)
            def _():
                y_copy(b, 0).start()
            return c

        def tail_drain(b, c):
            @pl.when(b >= total)
            def _():
                y_copy(b, 0).wait()
            return c

        n_blocks = ys_ref.shape[0] // MOE_BLK
        lax.fori_loop(0, n_blocks, tail_issue, 0)
        lax.fori_loop(0, n_blocks, tail_drain, 0)


def _moe_experts(cnt, ps, nblk, xs, w_gate, w_up, w_down):
    n_slots = xs.shape[0]
    w_map = lambda e, *_: (e, 0, 0)
    anyspec = pl.BlockSpec(memory_space=pl.ANY)
    grid_spec = pltpu.PrefetchScalarGridSpec(
        num_scalar_prefetch=3,
        grid=(N_EXPERTS,),
        in_specs=[pl.BlockSpec((None, D_MODEL, D_EXPERT), w_map),
                  pl.BlockSpec((None, D_MODEL, D_EXPERT), w_map),
                  pl.BlockSpec((None, D_EXPERT, D_MODEL), w_map),
                  anyspec],
        out_specs=anyspec,
        scratch_shapes=[pltpu.VMEM((D_MODEL, D_EXPERT), BF16), pltpu.VMEM((D_MODEL, D_EXPERT), BF16),
                        pltpu.VMEM((D_EXPERT, D_MODEL), BF16),
                        pltpu.VMEM((2, MOE_BLK, D_MODEL), F32), pltpu.VMEM((2, MOE_BLK, D_MODEL), F32),
                        pltpu.SemaphoreType.DMA((2,)), pltpu.SemaphoreType.DMA((2,))],
    )
    return pl.pallas_call(
        _moe_kernel,
        grid_spec=grid_spec,
        out_shape=jax.ShapeDtypeStruct((n_slots, D_MODEL), F32),
        compiler_params=_cparams(("arbitrary",), 48),
        name="moe_experts",
    )(cnt, ps, nblk, w_gate, w_up, w_down, xs)


def _combine_kernel(dcur_ref, dnxt_ref, gate_ref, h2_ref, x1_ref, g2_ref, wsg_ref, wsu_ref, wsd_ref,
                    ys_ref, o_ref, ybuf, sem):
    i = pl.program_id(0)
    n = pl.num_programs(0)
    tm = h2_ref.shape[0]
    slot = i % 2

    def row_copy(d, s, k, r):
        return pltpu.make_async_copy(ys_ref.at[pl.ds(d, 1)], ybuf.at[s, k, pl.ds(r, 1)], sem.at[s])

    def gather(dref, s, wait):
        def step(r, c):
            for k in range(TOP_K):
                cp = row_copy(dref[k, r], s, k, r)
                if wait:
                    cp.wait()
                else:
                    cp.start()
            return c

        lax.fori_loop(0, tm, step, 0, unroll=2)

    @pl.when(i == 0)
    def _():
        gather(dcur_ref, 0, False)

    for par in (0, 1):
        @pl.when((slot == par) & (i + 1 < n))
        def _(par=par):
            gather(dnxt_ref, 1 - par, False)

    for par in (0, 1):
        @pl.when(slot == par)
        def _(par=par):
            gather(dcur_ref, par, True)

    gates = gate_ref[...]
    routed = gates[:, 0:1] * ybuf[slot, 0]
    for k in range(1, TOP_K):
        routed = routed + gates[:, k:k + 1] * ybuf[slot, k]
    hb = h2_ref[...].astype(BF16)
    hid = jax.nn.silu(_dot(hb, wsg_ref[...])) * _dot(hb, wsu_ref[...])
    shared = _dot(hid.astype(BF16), wsd_ref[...])
    o_ref[...] = x1_ref[...] + g2_ref[...] * (routed + shared)


def _combine(dest, gate_tk, h2, x1, mod, wsg_bf, wsu_bf, wsd_bf, ys, *, tm, rows_per_mod):
    t = h2.shape[0]
    nt = t // tm
    rm = mod.shape[1]
    const = lambda shape: pl.BlockSpec(shape, lambda i: (0,) * len(shape))
    return pl.pallas_call(
        _combine_kernel,
        grid=(nt,),
        in_specs=[pl.BlockSpec((TOP_K, tm), lambda i: (0, i), memory_space=pltpu.SMEM),
                  pl.BlockSpec((TOP_K, tm), lambda i: (0, jnp.minimum(i + 1, nt - 1)), memory_space=pltpu.SMEM),
                  pl.BlockSpec((tm, TOP_K), lambda i: (i, 0)),
                  pl.BlockSpec((tm, D_MODEL), lambda i: (i, 0)),
                  pl.BlockSpec((tm, D_MODEL), lambda i: (i, 0)),
                  pl.BlockSpec((None, rm, D_MODEL), lambda i: ((i * tm) // rows_per_mod, 0, 5)),
                  const((D_MODEL, D_EXPERT)), const((D_MODEL, D_EXPERT)), const((D_EXPERT, D_MODEL)),
                  pl.BlockSpec(memory_space=pl.ANY)],
        out_specs=pl.BlockSpec((tm, D_MODEL), lambda i: (i, 0)),
        out_shape=jax.ShapeDtypeStruct((t, D_MODEL), F32),
        scratch_shapes=[pltpu.VMEM((2, TOP_K, tm, D_MODEL), F32), pltpu.SemaphoreType.DMA((2,))],
        compiler_params=_cparams(("arbitrary",), 48),
        name="moe_combine",
    )(dest, dest, gate_tk, h2, x1, mod, wsg_bf, wsu_bf, wsd_bf, ys)


def kernel(x_prompt, x_sample, cache_k, cache_v, state_ssm_re, state_ssm_im, c_prompt, c_sample,
           w_ada, b_ada, norm1_g, w_in, q_norm_g, k_norm_g, ssm_A_re, ssm_A_im, ssm_log_dt,
           ssm_B_re, ssm_B_im, ssm_C_re, ssm_C_im, ssm_D, ssm_w_glu, ssm_b_glu, attn_out_g,
           ssm_out_g, w_out, norm2_g, w_router, b_router, w_exp_gate, w_exp_up, w_exp_down,
           w_sh_gate, w_sh_up, w_sh_down):
    nb_p, seq, _ = x_prompt.shape
    nb_s = x_sample.shape[0]
    t_p = nb_p * seq
    l = 0

    n_c = nb_p + nb_s
    c_rows = -(-n_c // SUBLANES) * SUBLANES
    c_all = jnp.concatenate([c_prompt, c_sample, jnp.zeros((c_rows - n_c, D_MODEL), F32)], axis=0)
    mod = _ada_mod(c_all, w_ada[l], b_ada[l])
    mod_p = mod[:nb_p].reshape(nb_p, 1, 6 * D_MODEL)
    mod_s = mod[nb_p:n_c].reshape(1, nb_s, 6 * D_MODEL)

    w_in_bf = w_in[l].astype(BF16)
    w_out_bf = w_out[l].astype(BF16)
    wglu_bf = ssm_w_glu[l].astype(BF16)
    wr_t_bf = w_router[l].T.astype(BF16)
    wsg_bf = w_sh_gate[l].astype(BF16)
    wsu_bf = w_sh_up[l].astype(BF16)
    wsd_bf = w_sh_down[l].astype(BF16)
    ng1 = norm1_g[l].reshape(1, D_MODEL)
    ng2 = norm2_g[l].reshape(1, D_MODEL)
    qkg = jnp.stack([q_norm_g[l], k_norm_g[l]]).reshape(2, 1, HEAD_DIM)
    ag = attn_out_g[l].reshape(1, D_ATTN)
    og = ssm_out_g[l].reshape(1, D_SSM)
    ssm_d = ssm_D[l].reshape(1, D_SSM)
    bglu = ssm_b_glu[l].reshape(1, D_SSM)

    xp = x_prompt.reshape(t_p, D_MODEL)
    xs = x_sample.reshape(nb_s, D_MODEL)

    tm_p = 512
    rope_p = _rope_tables(seq, 0, 1)
    rope_s = _rope_tables(nb_s, PAST_LEN, 0)
    proj_p = _in_proj(xp, mod_p, ng1, w_in_bf, qkg, rope_p, tm=tm_p, rows_per_mod=seq,
                      rope_blocks=seq // tm_p)
    proj_s = _in_proj(xs, mod_s, ng1, w_in_bf, qkg, rope_s, tm=nb_s, rows_per_mod=nb_s, rope_blocks=1)

    attn_p = _attn_prompt(proj_p, nb_p)
    q_s = proj_s[:, :D_ATTN].reshape(nb_s, N_HEADS, HEAD_DIM)
    k_s = proj_s[:, D_ATTN:2 * D_ATTN].reshape(nb_s, N_HEADS, HEAD_DIM)
    v_s = proj_s[:, 2 * D_ATTN:3 * D_ATTN].reshape(nb_s, N_HEADS, HEAD_DIM)
    attn_s = _attn_sample(q_s, k_s, v_s, cache_k[l], cache_v[l]).reshape(nb_s, D_ATTN)
    k_new = _cache_update(cache_k[l], k_s)
    v_new = _cache_update(cache_v[l], v_s)

    abar, wb, wct = _ssm_prep(ssm_A_re[l], ssm_A_im[l], ssm_log_dt[l], ssm_B_re[l], ssm_B_im[l],
                              ssm_C_re[l], ssm_C_im[l])
    ssm_p, hre_p, him_p = _ssm_prompt(proj_p, nb_p, abar, wb, wct, ssm_d, wglu_bf, bglu, og)
    ssm_s, hre_s, him_s = _ssm_sample(proj_s[:, 3 * D_ATTN:], state_ssm_re[l].reshape(nb_s, N_STATE),
                                      state_ssm_im[l].reshape(nb_s, N_STATE), abar, wb, wct,
                                      ssm_d, wglu_bf, bglu, og)

    x1_p = _out_proj(attn_p, ssm_p, ag, w_out_bf, xp, mod_p, tm=tm_p, rows_per_mod=seq)
    x1_s = _out_proj(attn_s, ssm_s, ag, w_out_bf, xs, mod_s, tm=nb_s, rows_per_mod=nb_s)

    h2_p, te_p, gt_p = _route(x1_p, mod_p, ng2, wr_t_bf, b_router[l], tm=256, rows_per_mod=seq)
    h2_s, te_s, gt_s = _route(x1_s, mod_s, ng2, wr_t_bf, b_router[l], tm=nb_s, rows_per_mod=nb_s)
    t_pad = (t_p // DISP_TM + 1) * DISP_TM
    te = jnp.concatenate([te_p, te_s, jnp.full((TOP_K, t_pad - t_p - nb_s), -1, I32)], axis=1)
    dest, nblk, cnt, ps = _dispatch_tables(te)
    n_assign = (t_p + nb_s) * TOP_K
    n_slots = (-(-n_assign // MOE_BLK) + N_EXPERTS) * MOE_BLK
    xs_sorted = _scatter_tokens(dest, cnt, ps, nblk, h2_p, h2_s, n_slots)
    ys_sorted = _moe_experts(cnt, ps, nblk, xs_sorted, w_exp_gate[l], w_exp_up[l], w_exp_down[l])
    y_p = _combine(dest[:, :t_p], gt_p.T, h2_p, x1_p, mod_p, wsg_bf, wsu_bf, wsd_bf, ys_sorted,
                   tm=128, rows_per_mod=seq)
    y_s = _combine(dest[:, t_p:t_p + nb_s], gt_s.T, h2_s, x1_s, mod_s, wsg_bf, wsu_bf, wsd_bf, ys_sorted,
                   tm=nb_s, rows_per_mod=nb_s)

    keep = min(WBUF, seq)
    kv = proj_p.reshape(nb_p, seq, D_IN_PROJ)[:, seq - keep:, D_ATTN:3 * D_ATTN]
    k_p = kv[..., :D_ATTN].reshape(1, nb_p, keep, N_HEADS, HEAD_DIM)
    v_p = kv[..., D_ATTN:].reshape(1, nb_p, keep, N_HEADS, HEAD_DIM)
    st = lambda h, b: h.reshape(1, b, SSM_GROUPS, SSM_STATE)
    return (y_p.reshape(nb_p, seq, D_MODEL), y_s.reshape(nb_s, 1, D_MODEL), k_p, v_p,
            st(hre_p, nb_p), st(him_p, nb_p), k_new[None], v_new[None], st(hre_s, nb_s), st(him_s, nb_s))
```

```python
import functools
import math

import jax
import jax.numpy as jnp
from jax import lax
from jax.experimental import pallas as pl
from jax.experimental.pallas import tpu as pltpu

F32 = jnp.float32
BF16 = jnp.bfloat16
I32 = jnp.int32

D_MODEL = 2048
SEQ = 4096
PAST_LEN = 8192
D_ATTN = 1024
HEAD_DIM = 128
N_HEADS = 8
ROT_DIM = 32
ROPE_THETA = 500000.0
DILATIONS = (1, 4, 16)
SPAN = 128
WBUF = 2048
D_SSM = 1024
SSM_CH = 16
SSM_GROUPS = 64
SSM_STATE = 64
N_STATE = SSM_GROUPS * SSM_STATE
N_EXPERTS = 256
TOP_K = 8
N_EXPERT_GROUPS = 8
TOPK_GROUPS = 4
D_EXPERT = 512
ROUTE_SCALE = 2.5
EPS = 1e-6
D_IN_PROJ = 4096

MOE_BLK = 128
MOE_SHIFT = MOE_BLK.bit_length() - 1
LANES = 128
SUBLANES = 8
MIB = 1024 * 1024


def _cparams(sem, vmem_mib):
    return pltpu.CompilerParams(dimension_semantics=sem, vmem_limit_bytes=vmem_mib * MIB)


def _rms(x, g):
    return x * lax.rsqrt(jnp.mean(x * x, axis=-1, keepdims=True) + EPS) * g


def _dot(a, b):
    return jnp.dot(a, b, preferred_element_type=F32)


def _dot_nt(a, b):
    return lax.dot_general(a, b, (((1,), (1,)), ((), ())), preferred_element_type=F32)


def _ada_kernel(c_ref, w_ref, b_ref, o_ref):
    a = jax.nn.silu(c_ref[...]).astype(BF16)
    o_ref[...] = _dot(a, w_ref[...].astype(BF16)) + b_ref[...]


def _ada_mod(c_all, w_ada, b_ada):
    rows = c_all.shape[0]
    n_out = w_ada.shape[1]
    tn = 1024
    return pl.pallas_call(
        _ada_kernel,
        grid=(n_out // tn,),
        in_specs=[pl.BlockSpec((rows, D_MODEL), lambda j: (0, 0)),
                  pl.BlockSpec((D_MODEL, tn), lambda j: (0, j)),
                  pl.BlockSpec((1, tn), lambda j: (0, j))],
        out_specs=pl.BlockSpec((rows, tn), lambda j: (0, j)),
        out_shape=jax.ShapeDtypeStruct((rows, n_out), F32),
        compiler_params=_cparams(("arbitrary",), 40),
        name="ada_mod",
    )(c_all, w_ada, b_ada.reshape(1, n_out))


def _rope_kernel(c_ref, s1_ref, s2_ref, *, pos0, stride):
    n = c_ref.shape[0]
    lane = lax.broadcasted_iota(I32, (n, LANES), 1)
    row = lax.broadcasted_iota(I32, (n, LANES), 0) + pl.program_id(0) * n
    pos = (row * stride + pos0).astype(F32)
    half = ROT_DIM // 2
    fi = (lane & (half - 1)).astype(F32)
    inv = jnp.exp(-math.log(ROPE_THETA) * fi / half)
    ang = pos * inv
    cos = jnp.cos(ang)
    sin = jnp.sin(ang)
    c_ref[...] = jnp.where(lane < ROT_DIM, cos, 1.0)
    s1_ref[...] = jnp.where(lane < half, -sin, 0.0)
    s2_ref[...] = jnp.where((lane >= half) & (lane < ROT_DIM), sin, 0.0)


def _rope_tables(n_rows, pos0, stride):
    tr = min(n_rows, 512)
    spec = pl.BlockSpec((tr, LANES), lambda i: (i, 0))
    shp = jax.ShapeDtypeStruct((n_rows, LANES), F32)
    return pl.pallas_call(
        functools.partial(_rope_kernel, pos0=pos0, stride=stride),
        grid=(n_rows // tr,),
        in_specs=[],
        out_specs=[spec, spec, spec],
        out_shape=[shp, shp, shp],
        compiler_params=_cparams(("arbitrary",), 32),
        name="rope_tables",
    )()


def _inproj_kernel(x_ref, sc_ref, sh_ref, ng_ref, w_ref, qkg_ref, rc_ref, rs1_ref, rs2_ref,
                   o_ref, h_scr):
    n = pl.program_id(1)

    @pl.when(n == 0)
    def _():
        h = _rms(x_ref[...], ng_ref[...]) * (1.0 + sc_ref[...]) + sh_ref[...]
        h_scr[...] = h.astype(BF16)

    o_ref[...] = _dot(h_scr[...], w_ref[...])

    @pl.when(n < 2)
    def _():
        g = qkg_ref[n]
        c = rc_ref[...]
        s1 = rs1_ref[...]
        s2 = rs2_ref[...]
        for hh in range(N_HEADS):
            sl = slice(hh * HEAD_DIM, (hh + 1) * HEAD_DIM)
            y = _rms(o_ref[:, sl], g)
            o_ref[:, sl] = (y * c + pltpu.roll(y, HEAD_DIM - ROT_DIM // 2, 1) * s1
                            + pltpu.roll(y, ROT_DIM // 2, 1) * s2)


def _in_proj(x, mod, ng, w_bf, qkg, rope, *, tm, rows_per_mod, rope_blocks):
    t = x.shape[0]
    tn = 1024
    rm = mod.shape[1]
    mt = t // tm

    def mod_map(j):
        return lambda m, n: ((m * tm) // rows_per_mod, 0, j)

    rope_spec = pl.BlockSpec((tm, LANES), lambda m, n: (m % rope_blocks, 0))
    return pl.pallas_call(
        _inproj_kernel,
        grid=(mt, D_IN_PROJ // tn),
        in_specs=[pl.BlockSpec((tm, D_MODEL), lambda m, n: (m, 0)),
                  pl.BlockSpec((None, rm, D_MODEL), mod_map(1)),
                  pl.BlockSpec((None, rm, D_MODEL), mod_map(0)),
                  pl.BlockSpec((1, D_MODEL), lambda m, n: (0, 0)),
                  pl.BlockSpec((D_MODEL, tn), lambda m, n: (0, n)),
                  pl.BlockSpec((2, 1, HEAD_DIM), lambda m, n: (0, 0, 0)),
                  rope_spec, rope_spec, rope_spec],
        out_specs=pl.BlockSpec((tm, tn), lambda m, n: (m, n)),
        out_shape=jax.ShapeDtypeStruct((t, D_IN_PROJ), F32),
        scratch_shapes=[pltpu.VMEM((tm, D_MODEL), BF16)],
        compiler_params=_cparams(("arbitrary", "arbitrary"), 48),
        name="in_proj",
    )(x, mod, mod, ng, w_bf, qkg, *rope)


ATTN_GROUP = 2


def _attn_prompt_kernel(q_ref, k_ref, v_ref, o_ref, o0, o1, o2, l0, l1, l2):
    scale = 1.0 / math.sqrt(HEAD_DIM)
    qi = lax.broadcasted_iota(I32, (SPAN, 2 * SPAN), 0)
    kj = lax.broadcasted_iota(I32, (SPAN, 2 * SPAN), 1)
    is_prev = kj < SPAN
    prev_mask = is_prev & (kj >= qi)
    own_mask = (~is_prev) & (kj - SPAN <= qi)
    neg = -jnp.inf
    oaccs = (o0, o1, o2)
    laccs = (l0, l1, l2)

    for p, d in enumerate(DILATIONS):
        nbk = SEQ // (SPAN * d)
        shift = nbk.bit_length() - 1
        oacc = oaccs[p]
        lacc = laccs[p]

        def rows(start, d=d):
            return pl.ds(start, SPAN) if d == 1 else pl.ds(start, SPAN, stride=d)

        def body(it, carry, d=d, nbk=nbk, shift=shift, oacc=oacc, lacc=lacc, rows=rows):
            loaded = []
            for j in range(ATTN_GROUP):
                idx = it * ATTN_GROUP + j
                r = idx >> shift
                n = idx & (nbk - 1)
                start = r + n * (SPAN * d)
                pstart = jnp.maximum(start - SPAN * d, r)
                q = q_ref[rows(start), :]
                kk = jnp.concatenate([k_ref[rows(pstart), :], k_ref[rows(start), :]], axis=0)
                vv = jnp.concatenate([v_ref[rows(pstart), :], v_ref[rows(start), :]], axis=0)
                loaded.append((start, n, q, kk, vv))
            results = []
            for start, n, q, kk, vv in loaded:
                valid = own_mask | (prev_mask & (n > 0))
                s = jnp.where(valid, _dot_nt(q.astype(BF16), kk.astype(BF16)) * scale, neg)
                m = jnp.max(s, axis=-1, keepdims=True)
                pr = jnp.exp(s - m)
                den = jnp.sum(pr, axis=-1, keepdims=True)
                o = _dot(pr.astype(BF16), vv.astype(BF16)) / den
                results.append((start, o, m + jnp.log(den)))
            for start, o, lse in results:
                oacc[rows(start), :] = o
                lacc[rows(start), :] = jnp.broadcast_to(lse, (SPAN, HEAD_DIM))
            return carry

        lax.fori_loop(0, SEQ // SPAN // ATTN_GROUP, body, 0)

    ch = 512
    for c0 in range(0, SEQ, ch):
        sl = slice(c0, c0 + ch)
        la, lb, lc = l0[sl, :], l1[sl, :], l2[sl, :]
        mx = jnp.maximum(jnp.maximum(la, lb), lc)
        wa, wb, wc = jnp.exp(la - mx), jnp.exp(lb - mx), jnp.exp(lc - mx)
        o_ref[sl, :] = (wa * o0[sl, :] + wb * o1[sl, :] + wc * o2[sl, :]) / (wa + wb + wc)


def _attn_prompt(proj, batch):
    blk = (SEQ, HEAD_DIM)
    scr = pltpu.VMEM((SEQ, HEAD_DIM), F32)
    return pl.pallas_call(
        _attn_prompt_kernel,
        grid=(batch, N_HEADS),
        in_specs=[pl.BlockSpec(blk, lambda b, h: (b, h)),
                  pl.BlockSpec(blk, lambda b, h: (b, N_HEADS + h)),
                  pl.BlockSpec(blk, lambda b, h: (b, 2 * N_HEADS + h))],
        out_specs=pl.BlockSpec(blk, lambda b, h: (b, h)),
        out_shape=jax.ShapeDtypeStruct((batch * SEQ, D_ATTN), F32),
        scratch_shapes=[scr] * 6,
        compiler_params=_cparams(("arbitrary", "arbitrary"), 48),
        name="attn_prompt",
    )(proj, proj, proj)


def _attn_sample_kernel(q_ref, kn_ref, vn_ref, k1, k4, k16, v1, v4, v16, o_ref):
    scale = 1.0 / math.sqrt(HEAD_DIM)
    q = q_ref[...]
    kn = kn_ref[...]
    vn = vn_ref[...]
    s_new = jnp.sum(q * kn, axis=-1, keepdims=True) * scale
    outs = []
    lses = []
    for kc, vc in ((k1, v1), (k4, v4), (k16, v16)):
        kk = kc[...]
        vv = vc[...]
        s = jnp.sum(kk * q[None], axis=-1, keepdims=True) * scale
        m = jnp.maximum(jnp.max(s, axis=0), s_new)
        p = jnp.exp(s - m[None])
        p_new = jnp.exp(s_new - m)
        den = jnp.sum(p, axis=0) + p_new
        o = (jnp.sum(p * vv, axis=0) + p_new * vn) / den
        outs.append(o)
        lses.append(m + jnp.log(den))
    mx = jnp.maximum(jnp.maximum(lses[0], lses[1]), lses[2])
    ws = [jnp.exp(l - mx) for l in lses]
    o_ref[...] = (ws[0] * outs[0] + ws[1] * outs[1] + ws[2] * outs[2]) / (ws[0] + ws[1] + ws[2])


def _attn_sample(q, kn, vn, cache_k, cache_v):
    b = q.shape[0]
    row = pl.BlockSpec((None, N_HEADS, HEAD_DIM), lambda i: (i, 0, 0))
    views, specs = [], []
    for c in (cache_k, cache_v):
        for d in DILATIONS:
            views.append(c.reshape(b, WBUF // d, d, N_HEADS, HEAD_DIM))
            last = WBUF // d // SPAN - 1
            specs.append(pl.BlockSpec((None, SPAN, None, N_HEADS, HEAD_DIM),
                                      lambda i, last=last: (i, last, 0, 0, 0)))
    return pl.pallas_call(
        _attn_sample_kernel,
        grid=(b,),
        in_specs=[row, row, row] + specs,
        out_specs=row,
        out_shape=jax.ShapeDtypeStruct((b, N_HEADS, HEAD_DIM), F32),
        compiler_params=_cparams(("arbitrary",), 32),
        name="attn_sample",
    )(q, kn, vn, *views)


CACHE_CHUNK = 256


def _cache_kernel(c_ref, n_ref, o_ref):
    for r0 in range(0, WBUF - 1, CACHE_CHUNK):
        r1 = min(r0 + CACHE_CHUNK, WBUF - 1)
        o_ref[r0:r1] = c_ref[r0 + 1:r1 + 1]
    o_ref[WBUF - 1] = n_ref[...]


def _cache_update(cache, new):
    b = cache.shape[0]
    blk = pl.BlockSpec((None, WBUF, N_HEADS, HEAD_DIM), lambda i: (i, 0, 0, 0))
    return pl.pallas_call(
        _cache_kernel,
        grid=(b,),
        in_specs=[blk, pl.BlockSpec((None, N_HEADS, HEAD_DIM), lambda i: (i, 0, 0))],
        out_specs=blk,
        out_shape=jax.ShapeDtypeStruct(cache.shape, cache.dtype),
        compiler_params=_cparams(("arbitrary",), 40),
        name="cache_update",
    )(cache, new)


def _ssm_prep_kernel(ar_ref, ai_ref, ldt_ref, br_ref, bi_ref, cr_ref, ci_ref,
                     ab_ref, wb_ref, wc_ref):
    ar = ar_ref[...]
    ai = ai_ref[...]
    dt = jnp.exp(ldt_ref[...])
    mag = jnp.exp(dt * ar)
    abar_re = mag * jnp.cos(dt * ai)
    abar_im = mag * jnp.sin(dt * ai)
    nr = abar_re - 1.0
    den = ar * ar + ai * ai
    coef_re = (nr * ar + abar_im * ai) / den
    coef_im = (abar_im * ar - nr * ai) / den
    ab_ref[0] = abar_re
    ab_ref[1] = abar_im
    wb_ref[...] = jnp.zeros(wb_ref.shape, wb_ref.dtype)
    wc_ref[...] = jnp.zeros(wc_ref.shape, wc_ref.dtype)
    lane = lax.broadcasted_iota(I32, (SSM_CH, LANES), 1)
    gpt = 16
    for g in range(SSM_GROUPS):
        kt, gl = divmod(g, gpt)
        keep = (lane < SSM_STATE) if g % 2 == 0 else (lane >= SSM_STATE)
        cre = coef_re[g:g + 1, :]
        cim = coef_im[g:g + 1, :]
        bre = br_ref[g]
        bim = bi_ref[g]
        bb_re = jnp.where(keep, cre * bre - cim * bim, 0.0).astype(BF16)
        bb_im = jnp.where(keep, cre * bim + cim * bre, 0.0).astype(BF16)
        rs = slice(gl * SSM_CH, (gl + 1) * SSM_CH)
        lt = (gl // 2) * LANES
        wb_ref[kt, rs, lt:lt + LANES] = bb_re
        wb_ref[kt, rs, 1024 + lt:1024 + lt + LANES] = bb_im
        c_re = jnp.where(keep, cr_ref[g], 0.0).astype(BF16)
        c_im = jnp.where(keep, -ci_ref[g], 0.0).astype(BF16)
        wc_ref[kt, rs, lt:lt + LANES] = c_re
        wc_ref[kt, rs, 1024 + lt:1024 + lt + LANES] = c_im


def _ssm_prep(a_re, a_im, log_dt, b_re, b_im, c_re, c_im):
    dup = lambda x: jnp.concatenate([x, x], axis=-1)
    br_t = dup(jnp.swapaxes(b_re, 1, 2))
    bi_t = dup(jnp.swapaxes(b_im, 1, 2))
    shapes = [jax.ShapeDtypeStruct((2, SSM_GROUPS, LANES), F32),
              jax.ShapeDtypeStruct((4, 256, 2048), BF16),
              jax.ShapeDtypeStruct((4, 256, 2048), BF16)]
    ab, wb, wct = pl.pallas_call(
        _ssm_prep_kernel,
        out_shape=shapes,
        compiler_params=pltpu.CompilerParams(vmem_limit_bytes=40 * MIB),
        name="ssm_prep",
    )(dup(a_re), dup(a_im), log_dt.reshape(SSM_GROUPS, 1), br_t, bi_t, dup(c_re), dup(c_im))
    abar = ab[:, :, :SSM_STATE].reshape(2, N_STATE)
    return abar, wb, wct


def _ssm_bu(ub, wb_ref):
    return [_dot(ub[:, kt * 256:(kt + 1) * 256], wb_ref[kt]) for kt in range(4)]


def _ssm_tail(h_re_tiles, h_im_tiles, u, wc_ref, d_ref, wg_ref, bg_ref, og_ref):
    ys = []
    for kt in range(4):
        wct = wc_ref[kt]
        ys.append(_dot_nt(h_re_tiles[kt].astype(BF16), wct[:, :1024])
                  + _dot_nt(h_im_tiles[kt].astype(BF16), wct[:, 1024:]))
    y = jnp.concatenate(ys, axis=-1) + d_ref[...] * u
    g = jax.nn.gelu(y)
    z = _dot(g.astype(BF16), wg_ref[...]) + bg_ref[...]
    out = g * jax.nn.sigmoid(z)
    return _rms(out, og_ref[...])


SCAN_LANES = 512
SSM_TC = 256
SEG_LEN = SSM_TC // SUBLANES


def _split3(x):
    hi = x.astype(BF16)
    r1 = x - hi.astype(F32)
    mid = r1.astype(BF16)
    lo = (r1 - mid.astype(F32)).astype(BF16)
    return hi, mid, lo


def _ssm_prompt_kernel(u_ref, wb_ref, wc_ref, ab_ref, d_ref, wg_ref, bg_ref, og_ref,
                       o_ref, hre_ref, him_ref, bre, bim, cre, cim, pwr, pwi):
    c = pl.program_id(1)
    tc = SSM_TC
    lanes = SCAN_LANES

    def cmul(xr, xi, yr, yi):
        return xr * yr - xi * yi, xr * yi + xi * yr

    @pl.when(c == 0)
    def _():
        cre[...] = jnp.zeros(cre.shape, F32)
        cim[...] = jnp.zeros(cim.shape, F32)
        ar, ai = ab_ref[0:1, :], ab_ref[1:2, :]
        cur_r, cur_i = ar, ai
        for i in range(SEG_LEN):
            pwr[i * SUBLANES:(i + 1) * SUBLANES, :] = jnp.broadcast_to(cur_r, (SUBLANES, N_STATE))
            pwi[i * SUBLANES:(i + 1) * SUBLANES, :] = jnp.broadcast_to(cur_i, (SUBLANES, N_STATE))
            cur_r, cur_i = cmul(cur_r, cur_i, ar, ai)

    jj = lax.broadcasted_iota(I32, (tc, tc), 0)
    tt = lax.broadcasted_iota(I32, (tc, tc), 1)
    seg_shift = SEG_LEN.bit_length() - 1
    perm = (tt == ((jj & (SUBLANES - 1)) << seg_shift) + (jj >> 3)).astype(BF16)
    unperm = (tt == ((jj & (SEG_LEN - 1)) << 3) + (jj >> seg_shift)).astype(BF16)

    hi, mid, lo = _split3(u_ref[...])
    u_hi = _dot(perm, hi)
    u = u_hi + _dot(perm, mid) + _dot(perm, lo)
    bu = _ssm_bu(u_hi.astype(BF16), wb_ref)
    for kt in range(4):
        bre[:, kt * 1024:(kt + 1) * 1024] = bu[kt][:, :1024]
        bim[:, kt * 1024:(kt + 1) * 1024] = bu[kt][:, 1024:]

    row = lax.broadcasted_iota(I32, (SUBLANES, lanes), 0)
    zero = jnp.zeros((SUBLANES, lanes), F32)

    for lc in range(N_STATE // lanes):
        sl = slice(lc * lanes, (lc + 1) * lanes)
        ar = jnp.broadcast_to(ab_ref[0:1, sl], (SUBLANES, lanes))
        ai = jnp.broadcast_to(ab_ref[1:2, sl], (SUBLANES, lanes))

        def local_step(i, carry, sl=sl, ar=ar, ai=ai):
            xr, xi = carry
            off = pl.multiple_of(i * SUBLANES, SUBLANES)
            tr, ti = cmul(ar, ai, xr, xi)
            xr = tr + bre[pl.ds(off, SUBLANES), sl]
            xi = ti + bim[pl.ds(off, SUBLANES), sl]
            bre[pl.ds(off, SUBLANES), sl] = xr
            bim[pl.ds(off, SUBLANES), sl] = xi
            return xr, xi

        hr, hi_ = lax.fori_loop(0, SEG_LEN, local_step, (zero, zero), unroll=4)

        f1r = pwr[(SEG_LEN - 1) * SUBLANES:SEG_LEN * SUBLANES, sl]
        f1i = pwi[(SEG_LEN - 1) * SUBLANES:SEG_LEN * SUBLANES, sl]
        f2r, f2i = cmul(f1r, f1i, f1r, f1i)
        f4r, f4i = cmul(f2r, f2i, f2r, f2i)
        f8r, f8i = cmul(f4r, f4i, f4r, f4i)
        psr = jnp.ones((SUBLANES, lanes), F32)
        psi = zero
        for bit, (fr, fi) in ((1, (f1r, f1i)), (2, (f2r, f2i)), (4, (f4r, f4i)), (8, (f8r, f8i))):
            has = ((row + 1) & bit) != 0
            nr, ni = cmul(psr, psi, fr, fi)
            psr = jnp.where(has, nr, psr)
            psi = jnp.where(has, ni, psi)
        for s, fr, fi in ((1, f1r, f1i), (2, f2r, f2i), (4, f4r, f4i)):
            sr = jnp.where(row >= s, pltpu.roll(hr, s, 0), 0.0)
            si = jnp.where(row >= s, pltpu.roll(hi_, s, 0), 0.0)
            tr, ti = cmul(fr, fi, sr, si)
            hr = hr + tr
            hi_ = hi_ + ti
        cr, ci = cre[:, sl], cim[:, sl]
        tr, ti = cmul(psr, psi, cr, ci)
        hr = hr + tr
        hi_ = hi_ + ti
        pr = jnp.where(row == 0, cr, pltpu.roll(hr, 1, 0))
        pi = jnp.where(row == 0, ci, pltpu.roll(hi_, 1, 0))
        cre[:, sl] = jnp.broadcast_to(hr[SUBLANES - 1:SUBLANES, :], (SUBLANES, lanes))
        cim[:, sl] = jnp.broadcast_to(hi_[SUBLANES - 1:SUBLANES, :], (SUBLANES, lanes))

        def carry_step(i, carry, sl=sl, pr=pr, pi=pi):
            off = pl.multiple_of(i * SUBLANES, SUBLANES)
            tr, ti = cmul(pwr[pl.ds(off, SUBLANES), sl], pwi[pl.ds(off, SUBLANES), sl], pr, pi)
            bre[pl.ds(off, SUBLANES), sl] = bre[pl.ds(off, SUBLANES), sl] + tr
            bim[pl.ds(off, SUBLANES), sl] = bim[pl.ds(off, SUBLANES), sl] + ti
            return carry

        lax.fori_loop(0, SEG_LEN, carry_step, 0, unroll=4)

    h_re = [bre[:, kt * 1024:(kt + 1) * 1024] for kt in range(4)]
    h_im = [bim[:, kt * 1024:(kt + 1) * 1024] for kt in range(4)]
    out = _ssm_tail(h_re, h_im, u, wc_ref, d_ref, wg_ref, bg_ref, og_ref).astype(BF16)
    o_ref[...] = _dot(unperm, out).astype(BF16)

    @pl.when(c == pl.num_programs(1) - 1)
    def _():
        hre_ref[...] = cre[0:1, :]
        him_ref[...] = cim[0:1, :]


def _ssm_prompt(proj, batch, abar, wb, wct, ssm_d, wglu_bf, bglu, og):
    tc = SSM_TC
    nchunk = SEQ // tc
    const2 = lambda shape: pl.BlockSpec(shape, lambda b, c: (0,) * len(shape))
    st_spec = pl.BlockSpec((None, 1, N_STATE), lambda b, c: (b, 0, 0))
    st_shape = jax.ShapeDtypeStruct((batch, 1, N_STATE), F32)
    return pl.pallas_call(
        _ssm_prompt_kernel,
        grid=(batch, nchunk),
        in_specs=[pl.BlockSpec((tc, D_SSM), lambda b, c: (b * nchunk + c, 3)),
                  const2((4, 256, 2048)), const2((4, 256, 2048)), const2((2, N_STATE)),
                  const2((1, D_SSM)), const2((D_SSM, D_SSM)), const2((1, D_SSM)), const2((1, D_SSM))],
        out_specs=[pl.BlockSpec((tc, D_SSM), lambda b, c: (b * nchunk + c, 0)), st_spec, st_spec],
        out_shape=[jax.ShapeDtypeStruct((batch * SEQ, D_SSM), BF16), st_shape, st_shape],
        scratch_shapes=[pltpu.VMEM((tc, N_STATE), F32), pltpu.VMEM((tc, N_STATE), F32),
                        pltpu.VMEM((SUBLANES, N_STATE), F32), pltpu.VMEM((SUBLANES, N_STATE), F32),
                        pltpu.VMEM((tc, N_STATE), F32), pltpu.VMEM((tc, N_STATE), F32)],
        compiler_params=_cparams(("arbitrary", "arbitrary"), 56),
        name="ssm_prompt",
    )(proj, wb, wct, abar, ssm_d, wglu_bf, bglu, og)


def _ssm_sample_kernel(u_ref, h0r_ref, h0i_ref, wb_ref, wc_ref, ab_ref, d_ref, wg_ref, bg_ref, og_ref,
                       o_ref, hre_ref, him_ref):
    u = u_ref[...]
    bu = _ssm_bu(u.astype(BF16), wb_ref)
    h_re, h_im = [], []
    for kt in range(4):
        sl = slice(kt * 1024, (kt + 1) * 1024)
        ar = ab_ref[0:1, sl]
        ai = ab_ref[1:2, sl]
        h0r = h0r_ref[:, sl]
        h0i = h0i_ref[:, sl]
        hr = bu[kt][:, :1024] + (ar * h0r - ai * h0i)
        hi = bu[kt][:, 1024:] + (ar * h0i + ai * h0r)
        hre_ref[:, sl] = hr
        him_ref[:, sl] = hi
        h_re.append(hr)
        h_im.append(hi)
    o_ref[...] = _ssm_tail(h_re, h_im, u, wc_ref, d_ref, wg_ref, bg_ref, og_ref).astype(BF16)


def _ssm_sample(u, h0_re, h0_im, abar, wb, wct, ssm_d, wglu_bf, bglu, og):
    b = u.shape[0]
    st_shape = jax.ShapeDtypeStruct((b, N_STATE), F32)
    return pl.pallas_call(
        _ssm_sample_kernel,
        out_shape=[jax.ShapeDtypeStruct((b, D_SSM), BF16), st_shape, st_shape],
        compiler_params=pltpu.CompilerParams(vmem_limit_bytes=40 * MIB),
        name="ssm_sample",
    )(u, h0_re, h0_im, wb, wct, abar, ssm_d, wglu_bf, bglu, og)


def _outproj_kernel(attn_ref, ssm_ref, ag_ref, w_ref, x_ref, g1_ref, o_ref, lhs):
    n = pl.program_id(1)

    @pl.when(n == 0)
    def _():
        lhs[:, :D_ATTN] = _rms(attn_ref[...], ag_ref[...]).astype(BF16)
        lhs[:, D_ATTN:] = ssm_ref[...]

    o_ref[...] = x_ref[...] + g1_ref[...] * _dot(lhs[...], w_ref[...])


def _out_proj(attn, ssm_n, ag, w_bf, x, mod, *, tm, rows_per_mod):
    t = x.shape[0]
    tn = 1024
    rm = mod.shape[1]
    return pl.pallas_call(
        _outproj_kernel,
        grid=(t // tm, D_MODEL // tn),
        in_specs=[pl.BlockSpec((tm, D_ATTN), lambda m, n: (m, 0)),
                  pl.BlockSpec((tm, D_SSM), lambda m, n: (m, 0)),
                  pl.BlockSpec((1, D_ATTN), lambda m, n: (0, 0)),
                  pl.BlockSpec((D_MODEL, tn), lambda m, n: (0, n)),
                  pl.BlockSpec((tm, tn), lambda m, n: (m, n)),
                  pl.BlockSpec((None, rm, tn), lambda m, n: ((m * tm) // rows_per_mod, 0, 2 * (D_MODEL // tn) + n))],
        out_specs=pl.BlockSpec((tm, tn), lambda m, n: (m, n)),
        out_shape=jax.ShapeDtypeStruct((t, D_MODEL), F32),
        scratch_shapes=[pltpu.VMEM((tm, D_MODEL), BF16)],
        compiler_params=_cparams(("arbitrary", "arbitrary"), 48),
        name="out_proj",
    )(attn, ssm_n, ag, w_bf, x, mod)


def _route_kernel(x_ref, sc_ref, sh_ref, ng_ref, wr_ref, br_ref, h2_ref, te_ref, gt_ref):
    tm = x_ref.shape[0]
    h2 = _rms(x_ref[...], ng_ref[...]) * (1.0 + sc_ref[...]) + sh_ref[...]
    h2_ref[...] = h2
    scores = jax.nn.sigmoid(_dot_nt(wr_ref[...], h2.astype(BF16)))
    biased = scores + br_ref[...]
    neg = -jnp.inf
    per_g = N_EXPERTS // N_EXPERT_GROUPS
    gi_iota = lax.broadcasted_iota(I32, (per_g, tm), 0)
    gscore = []
    for g in range(N_EXPERT_GROUPS):
        xg = biased[g * per_g:(g + 1) * per_g, :]
        m1 = jnp.max(xg, axis=0, keepdims=True)
        i1 = jnp.min(jnp.where(xg == m1, gi_iota, per_g), axis=0, keepdims=True)
        m2 = jnp.max(jnp.where(gi_iota == i1, neg, xg), axis=0, keepdims=True)
        gscore.append(m1 + m2)
    parts = []
    for i in range(N_EXPERT_GROUPS):
        rank = jnp.zeros((1, tm), I32)
        for j in range(N_EXPERT_GROUPS):
            if j == i:
                continue
            ahead = (gscore[j] >= gscore[i]) if j < i else (gscore[j] > gscore[i])
            rank = rank + ahead.astype(I32)
        parts.append(jnp.where(rank < TOPK_GROUPS, biased[i * per_g:(i + 1) * per_g, :], neg))
    cur = jnp.concatenate(parts, axis=0)
    e_iota = lax.broadcasted_iota(I32, (N_EXPERTS, tm), 0)
    idxs, gates = [], []
    for _ in range(TOP_K):
        m = jnp.max(cur, axis=0, keepdims=True)
        idx = jnp.min(jnp.where(cur == m, e_iota, N_EXPERTS), axis=0, keepdims=True)
        hit = e_iota == idx
        gates.append(jnp.sum(jnp.where(hit, scores, 0.0), axis=0, keepdims=True))
        idxs.append(idx)
        cur = jnp.where(hit, neg, cur)
    tot = gates[0]
    for g in gates[1:]:
        tot = tot + g
    for k in range(TOP_K):
        te_ref[k:k + 1, :] = idxs[k]
        gt_ref[k:k + 1, :] = gates[k] / tot * ROUTE_SCALE


def _route(x1, mod, ng, wr_t_bf, b_router, *, tm, rows_per_mod):
    t = x1.shape[0]
    rm = mod.shape[1]

    def mod_map(j):
        return lambda m: ((m * tm) // rows_per_mod, 0, j)

    return pl.pallas_call(
        _route_kernel,
        grid=(t // tm,),
        in_specs=[pl.BlockSpec((tm, D_MODEL), lambda m: (m, 0)),
                  pl.BlockSpec((None, rm, D_MODEL), mod_map(4)),
                  pl.BlockSpec((None, rm, D_MODEL), mod_map(3)),
                  pl.BlockSpec((1, D_MODEL), lambda m: (0, 0)),
                  pl.BlockSpec((N_EXPERTS, D_MODEL), lambda m: (0, 0)),
                  pl.BlockSpec((N_EXPERTS, 1), lambda m: (0, 0))],
        out_specs=[pl.BlockSpec((tm, D_MODEL), lambda m: (m, 0)),
                   pl.BlockSpec((TOP_K, tm), lambda m: (0, m)),
                   pl.BlockSpec((TOP_K, tm), lambda m: (0, m))],
        out_shape=[jax.ShapeDtypeStruct((t, D_MODEL), F32),
                   jax.ShapeDtypeStruct((TOP_K, t), I32),
                   jax.ShapeDtypeStruct((TOP_K, t), F32)],
        compiler_params=_cparams(("arbitrary",), 40),
        name="route",
    )(x1, mod, mod, ng, wr_t_bf, b_router.reshape(N_EXPERTS, 1))


DISP_TM = 256


def _rank_kernel(te_ref, rank_ref, cnt_ref, run):
    i = pl.program_id(0)
    tm = te_ref.shape[1]

    @pl.when(i == 0)
    def _():
        run[...] = jnp.zeros(run.shape, F32)

    te = te_ref[...]
    e_iota = lax.broadcasted_iota(I32, (N_EXPERTS, tm), 0)
    a = jnp.zeros((N_EXPERTS, tm), F32)
    for k in range(TOP_K):
        a = a + (e_iota == te[k:k + 1, :]).astype(F32)
    upper = (lax.broadcasted_iota(I32, (tm, tm), 0) < lax.broadcasted_iota(I32, (tm, tm), 1)).astype(BF16)
    tot = _dot(a.astype(BF16), upper) + run[:, 0:1]
    for k in range(TOP_K):
        rk = jnp.sum(jnp.where(e_iota == te[k:k + 1, :], tot, 0.0), axis=0, keepdims=True)
        rank_ref[k:k + 1, :] = rk.astype(I32)
    run[...] = run[...] + jnp.sum(a, axis=1, keepdims=True)

    @pl.when(i == pl.num_programs(0) - 1)
    def _():
        cnt_ref[...] = run[...].astype(I32)


def _dest_kernel(te_ref, rank_ref, cnt_ref, dest_ref, ps_ref, nblk_ref, ps_scr):
    i = pl.program_id(0)
    tm = te_ref.shape[1]

    @pl.when(i == 0)
    def _():
        nb_e = ((cnt_ref[...] + (MOE_BLK - 1)) >> MOE_SHIFT).astype(F32)
        lower = (lax.broadcasted_iota(I32, (N_EXPERTS, N_EXPERTS), 1)
                 < lax.broadcasted_iota(I32, (N_EXPERTS, N_EXPERTS), 0)).astype(BF16)
        ps_b = _dot(lower, nb_e.astype(BF16))
        ps_scr[...] = ps_b * MOE_BLK
        ps_ref[...] = (ps_b * MOE_BLK).astype(I32)
        nblk_ref[...] = jnp.max(ps_b + nb_e, axis=0, keepdims=True).astype(I32)

    te = te_ref[...]
    rk = rank_ref[...]
    ps = ps_scr[:, 0:1]
    e_iota = lax.broadcasted_iota(I32, (N_EXPERTS, tm), 0)
    for k in range(TOP_K):
        tek = te[k:k + 1, :]
        base = jnp.sum(jnp.where(e_iota == tek, ps, 0.0), axis=0, keepdims=True)
        dest_ref[k:k + 1, :] = jnp.where(tek >= 0, base.astype(I32) + rk[k:k + 1, :], -1)


def _dispatch_tables(te):
    tpad = te.shape[1]
    tm = DISP_TM
    tile = pl.BlockSpec((TOP_K, tm), lambda i: (0, i))
    full = lambda shape: pl.BlockSpec(shape, lambda i: (0,) * len(shape))
    rank, cnt = pl.pallas_call(
        _rank_kernel,
        grid=(tpad // tm,),
        in_specs=[tile],
        out_specs=[tile, full((N_EXPERTS, LANES))],
        out_shape=[jax.ShapeDtypeStruct((TOP_K, tpad), I32),
                   jax.ShapeDtypeStruct((N_EXPERTS, LANES), I32)],
        scratch_shapes=[pltpu.VMEM((N_EXPERTS, LANES), F32)],
        compiler_params=_cparams(("arbitrary",), 32),
        name="moe_rank",
    )(te)
    dest, ps, nblk = pl.pallas_call(
        _dest_kernel,
        grid=(tpad // tm,),
        in_specs=[tile, tile, full((N_EXPERTS, LANES))],
        out_specs=[tile, full((N_EXPERTS, LANES)), full((1, LANES))],
        out_shape=[jax.ShapeDtypeStruct((TOP_K, tpad), I32),
                   jax.ShapeDtypeStruct((N_EXPERTS, LANES), I32),
                   jax.ShapeDtypeStruct((1, LANES), I32)],
        scratch_shapes=[pltpu.VMEM((N_EXPERTS, LANES), F32)],
        compiler_params=_cparams(("arbitrary",), 32),
        name="moe_dest",
    )(te, rank, cnt)
    return dest, nblk[0, :1], cnt[:, 0], ps[:, 0]


PAD_BITS = (64, 32, 16, 8)


def _scatter_kernel(cnt_ref, ps_ref, nblk_ref, dest_ref, hp_ref, hs_ref, xs_ref, zeros, sem, zsem, *,
                    n_prompt_tiles):
    i = pl.program_id(0)
    tm = dest_ref.shape[1]

    def row_copy(src_ref, r, d):
        return pltpu.make_async_copy(src_ref.at[pl.ds(r, 1)], xs_ref.at[pl.ds(d, 1)], sem)

    def scatter_rows(src_ref, n_rows):
        def issue(r, c):
            for k in range(TOP_K):
                row_copy(src_ref, r, dest_ref[k, r]).start()
            return c

        def drain(r, c):
            for k in range(TOP_K):
                row_copy(src_ref, r, dest_ref[k, r]).wait()
            return c

        lax.fori_loop(0, n_rows, issue, 0, unroll=2)
        lax.fori_loop(0, n_rows, drain, 0, unroll=2)

    @pl.when(i < n_prompt_tiles)
    def _():
        scatter_rows(hp_ref, tm)

    @pl.when(i == n_prompt_tiles)
    def _():
        scatter_rows(hs_ref, hs_ref.shape[0])

    @pl.when(i == 0)
    def _():
        zeros[...] = jnp.zeros(zeros.shape, F32)

        def pad_copies(e, do):
            n = cnt_ref[e]
            npad = (((n + (MOE_BLK - 1)) >> MOE_SHIFT) << MOE_SHIFT) - n
            base = ps_ref[e] + n
            head = npad & (SUBLANES - 1)
            for s in range(SUBLANES - 1):
                @pl.when(s < head)
                def _(s=s):
                    do(pltpu.make_async_copy(zeros.at[pl.ds(0, 1)], xs_ref.at[pl.ds(base + s, 1)], zsem))
            off = base + head
            for bit in PAD_BITS:
                take = (npad & bit) != 0

                @pl.when(take)
                def _(off=off, bit=bit):
                    dst = xs_ref.at[pl.ds(pl.multiple_of(off, SUBLANES), bit)]
                    do(pltpu.make_async_copy(zeros.at[pl.ds(0, bit)], dst, zsem))
                off = off + jnp.where(take, bit, 0)

        def issue(e, c):
            pad_copies(e, lambda cp: cp.start())
            return c

        def drain(e, c):
            pad_copies(e, lambda cp: cp.wait())
            return c

        lax.fori_loop(0, N_EXPERTS, issue, 0)
        lax.fori_loop(0, N_EXPERTS, drain, 0)

        def tail_copy(b):
            dst = xs_ref.at[pl.ds(pl.multiple_of(b * MOE_BLK, MOE_BLK), MOE_BLK)]
            return pltpu.make_async_copy(zeros, dst, zsem)

        def tail_issue(b, c):
            @pl.when(b >= nblk_ref[0])
            def _():
                tail_copy(b).start()
            return c

        def tail_drain(b, c):
            @pl.when(b >= nblk_ref[0])
            def _():
                tail_copy(b).wait()
            return c

        n_blocks = xs_ref.shape[0] // MOE_BLK
        lax.fori_loop(0, n_blocks, tail_issue, 0)
        lax.fori_loop(0, n_blocks, tail_drain, 0)


def _scatter_tokens(dest, cnt, ps, nblk, h2_p, h2_s, n_slots):
    tm = DISP_TM
    n_prompt_tiles = h2_p.shape[0] // tm
    grid_spec = pltpu.PrefetchScalarGridSpec(
        num_scalar_prefetch=3,
        grid=(n_prompt_tiles + 1,),
        in_specs=[pl.BlockSpec((TOP_K, tm), lambda i, *_: (0, i), memory_space=pltpu.SMEM),
                  pl.BlockSpec((tm, D_MODEL), lambda i, *_: (jnp.minimum(i, n_prompt_tiles - 1), 0)),
                  pl.BlockSpec(h2_s.shape, lambda i, *_: (0, 0))],
        out_specs=pl.BlockSpec(memory_space=pl.ANY),
        scratch_shapes=[pltpu.VMEM((MOE_BLK, D_MODEL), F32),
                        pltpu.SemaphoreType.DMA(()), pltpu.SemaphoreType.DMA(())],
    )
    return pl.pallas_call(
        functools.partial(_scatter_kernel, n_prompt_tiles=n_prompt_tiles),
        grid_spec=grid_spec,
        out_shape=jax.ShapeDtypeStruct((n_slots, D_MODEL), F32),
        compiler_params=_cparams(("arbitrary",), 32),
        name="moe_scatter",
    )(cnt, ps, nblk, dest, h2_p, h2_s)


ROW_DMA_PRIORITY = 1

def _moe_kernel(cnt_ref, ps_ref, nblk_ref, wg_ref, wu_ref, wd_ref, xs_ref, ys_ref,
                wgb, wub, wdb, xbuf, ybuf, xsem, ysem):
    e = pl.program_id(0)
    total = nblk_ref[0]
    nb = (cnt_ref[e] + (MOE_BLK - 1)) >> MOE_SHIFT
    g0 = ps_ref[e] >> MOE_SHIFT

    def rows(g):
        return pl.ds(pl.multiple_of(g * MOE_BLK, MOE_BLK), MOE_BLK)

    def x_copy(g, slot):
        return pltpu.make_async_copy(xs_ref.at[rows(g)], xbuf.at[slot], xsem.at[slot])

    def y_copy(g, slot):
        return pltpu.make_async_copy(ybuf.at[slot], ys_ref.at[rows(g)], ysem.at[slot])

    @pl.when(e == 0)
    def _():
        x_copy(0, 0).start(priority=ROW_DMA_PRIORITY)

    @pl.when(nb > 0)
    def _():
        wgb[...] = wg_ref[...].astype(BF16)
        wub[...] = wu_ref[...].astype(BF16)
        wdb[...] = wd_ref[...].astype(BF16)

        def body(c, carry):
            g = g0 + c
            slot = g & 1

            @pl.when(g + 1 < total)
            def _():
                x_copy(g + 1, 1 - slot).start(priority=ROW_DMA_PRIORITY)

            x_copy(g, slot).wait()

            @pl.when(g >= 2)
            def _():
                y_copy(g - 2, slot).wait()

            x = xbuf[slot].astype(BF16)
            hid = jax.nn.silu(_dot(x, wgb[...])) * _dot(x, wub[...])
            ybuf[slot] = _dot(hid.astype(BF16), wdb[...])
            y_copy(g, slot).start(priority=ROW_DMA_PRIORITY)
            return carry

        lax.fori_loop(0, nb, body, 0)

    @pl.when(e == pl.num_programs(0) - 1)
    def _():
        @pl.when(total >= 2)
        def _():
            y_copy(total - 2, total & 1).wait()

        y_copy(total - 1, (total - 1) & 1).wait()

        ybuf[0] = jnp.zeros((MOE_BLK, D_MODEL), F32)

        def tail_issue(b, c):
            @pl.when(b >= total)
            def _():
                y_copy(b, 0).start()
            return c

        def tail_drain(b, c):
            @pl.when(b >= total)
            def _():
                y_copy(b, 0).wait()
            return c

        n_blocks = ys_ref.shape[0] // MOE_BLK
        lax.fori_loop(0, n_blocks, tail_issue, 0)
        lax.fori_loop(0, n_blocks, tail_drain, 0)


def _moe_experts(cnt, ps, nblk, xs, w_gate, w_up, w_down):
    n_slots = xs.shape[0]
    w_map = lambda e, *_: (e, 0, 0)
    anyspec = pl.BlockSpec(memory_space=pl.ANY)
    grid_spec = pltpu.PrefetchScalarGridSpec(
        num_scalar_prefetch=3,
        grid=(N_EXPERTS,),
        in_specs=[pl.BlockSpec((None, D_MODEL, D_EXPERT), w_map),
                  pl.BlockSpec((None, D_MODEL, D_EXPERT), w_map),
                  pl.BlockSpec((None, D_EXPERT, D_MODEL), w_map),
                  anyspec],
        out_specs=anyspec,
        scratch_shapes=[pltpu.VMEM((D_MODEL, D_EXPERT), BF16), pltpu.VMEM((D_MODEL, D_EXPERT), BF16),
                        pltpu.VMEM((D_EXPERT, D_MODEL), BF16),
                        pltpu.VMEM((2, MOE_BLK, D_MODEL), F32), pltpu.VMEM((2, MOE_BLK, D_MODEL), F32),
                        pltpu.SemaphoreType.DMA((2,)), pltpu.SemaphoreType.DMA((2,))],
    )
    return pl.pallas_call(
        _moe_kernel,
        grid_spec=grid_spec,
        out_shape=jax.ShapeDtypeStruct((n_slots, D_MODEL), F32),
        compiler_params=_cparams(("arbitrary",), 48),
        name="moe_experts",
    )(cnt, ps, nblk, w_gate, w_up, w_down, xs)


def _combine_kernel(dcur_ref, dnxt_ref, gate_ref, h2_ref, x1_ref, g2_ref, wsg_ref, wsu_ref, wsd_ref,
                    ys_ref, o_ref, ybuf, sem):
    i = pl.program_id(0)
    n = pl.num_programs(0)
    tm = h2_ref.shape[0]
    slot = i % 2

    def row_copy(d, s, k, r):
        return pltpu.make_async_copy(ys_ref.at[pl.ds(d, 1)], ybuf.at[s, k, pl.ds(r, 1)], sem.at[s])

    def gather(dref, s, wait):
        def step(r, c):
            for k in range(TOP_K):
                cp = row_copy(dref[k, r], s, k, r)
                if wait:
                    cp.wait()
                else:
                    cp.start()
            return c

        lax.fori_loop(0, tm, step, 0, unroll=2)

    @pl.when(i == 0)
    def _():
        gather(dcur_ref, 0, False)

    for par in (0, 1):
        @pl.when((slot == par) & (i + 1 < n))
        def _(par=par):
            gather(dnxt_ref, 1 - par, False)

    for par in (0, 1):
        @pl.when(slot == par)
        def _(par=par):
            gather(dcur_ref, par, True)

    gates = gate_ref[...]
    routed = gates[:, 0:1] * ybuf[slot, 0]
    for k in range(1, TOP_K):
        routed = routed + gates[:, k:k + 1] * ybuf[slot, k]
    hb = h2_ref[...].astype(BF16)
    hid = jax.nn.silu(_dot(hb, wsg_ref[...])) * _dot(hb, wsu_ref[...])
    shared = _dot(hid.astype(BF16), wsd_ref[...])
    o_ref[...] = x1_ref[...] + g2_ref[...] * (routed + shared)


def _combine(dest, gate_tk, h2, x1, mod, wsg_bf, wsu_bf, wsd_bf, ys, *, tm, rows_per_mod):
    t = h2.shape[0]
    nt = t // tm
    rm = mod.shape[1]
    const = lambda shape: pl.BlockSpec(shape, lambda i: (0,) * len(shape))
    return pl.pallas_call(
        _combine_kernel,
        grid=(nt,),
        in_specs=[pl.BlockSpec((TOP_K, tm), lambda i: (0, i), memory_space=pltpu.SMEM),
                  pl.BlockSpec((TOP_K, tm), lambda i: (0, jnp.minimum(i + 1, nt - 1)), memory_space=pltpu.SMEM),
                  pl.BlockSpec((tm, TOP_K), lambda i: (i, 0)),
                  pl.BlockSpec((tm, D_MODEL), lambda i: (i, 0)),
                  pl.BlockSpec((tm, D_MODEL), lambda i: (i, 0)),
                  pl.BlockSpec((None, rm, D_MODEL), lambda i: ((i * tm) // rows_per_mod, 0, 5)),
                  const((D_MODEL, D_EXPERT)), const((D_MODEL, D_EXPERT)), const((D_EXPERT, D_MODEL)),
                  pl.BlockSpec(memory_space=pl.ANY)],
        out_specs=pl.BlockSpec((tm, D_MODEL), lambda i: (i, 0)),
        out_shape=jax.ShapeDtypeStruct((t, D_MODEL), F32),
        scratch_shapes=[pltpu.VMEM((2, TOP_K, tm, D_MODEL), F32), pltpu.SemaphoreType.DMA((2,))],
        compiler_params=_cparams(("arbitrary",), 48),
        name="moe_combine",
    )(dest, dest, gate_tk, h2, x1, mod, wsg_bf, wsu_bf, wsd_bf, ys)


def kernel(x_prompt, x_sample, cache_k, cache_v, state_ssm_re, state_ssm_im, c_prompt, c_sample,
           w_ada, b_ada, norm1_g, w_in, q_norm_g, k_norm_g, ssm_A_re, ssm_A_im, ssm_log_dt,
           ssm_B_re, ssm_B_im, ssm_C_re, ssm_C_im, ssm_D, ssm_w_glu, ssm_b_glu, attn_out_g,
           ssm_out_g, w_out, norm2_g, w_router, b_router, w_exp_gate, w_exp_up, w_exp_down,
           w_sh_gate, w_sh_up, w_sh_down):
    nb_p, seq, _ = x_prompt.shape
    nb_s = x_sample.shape[0]
    t_p = nb_p * seq
    l = 0

    n_c = nb_p + nb_s
    c_rows = -(-n_c // SUBLANES) * SUBLANES
    c_all = jnp.concatenate([c_prompt, c_sample, jnp.zeros((c_rows - n_c, D_MODEL), F32)], axis=0)
    mod = _ada_mod(c_all, w_ada[l], b_ada[l])
    mod_p = mod[:nb_p].reshape(nb_p, 1, 6 * D_MODEL)
    mod_s = mod[nb_p:n_c].reshape(1, nb_s, 6 * D_MODEL)

    w_in_bf = w_in[l].astype(BF16)
    w_out_bf = w_out[l].astype(BF16)
    wglu_bf = ssm_w_glu[l].astype(BF16)
    wr_t_bf = w_router[l].T.astype(BF16)
    wsg_bf = w_sh_gate[l].astype(BF16)
    wsu_bf = w_sh_up[l].astype(BF16)
    wsd_bf = w_sh_down[l].astype(BF16)
    ng1 = norm1_g[l].reshape(1, D_MODEL)
    ng2 = norm2_g[l].reshape(1, D_MODEL)
    qkg = jnp.stack([q_norm_g[l], k_norm_g[l]]).reshape(2, 1, HEAD_DIM)
    ag = attn_out_g[l].reshape(1, D_ATTN)
    og = ssm_out_g[l].reshape(1, D_SSM)
    ssm_d = ssm_D[l].reshape(1, D_SSM)
    bglu = ssm_b_glu[l].reshape(1, D_SSM)

    xp = x_prompt.reshape(t_p, D_MODEL)
    xs = x_sample.reshape(nb_s, D_MODEL)

    tm_p = 512
    rope_p = _rope_tables(seq, 0, 1)
    rope_s = _rope_tables(nb_s, PAST_LEN, 0)
    proj_p = _in_proj(xp, mod_p, ng1, w_in_bf, qkg, rope_p, tm=tm_p, rows_per_mod=seq,
                      rope_blocks=seq // tm_p)
    proj_s = _in_proj(xs, mod_s, ng1, w_in_bf, qkg, rope_s, tm=nb_s, rows_per_mod=nb_s, rope_blocks=1)

    attn_p = _attn_prompt(proj_p, nb_p)
    q_s = proj_s[:, :D_ATTN].reshape(nb_s, N_HEADS, HEAD_DIM)
    k_s = proj_s[:, D_ATTN:2 * D_ATTN].reshape(nb_s, N_HEADS, HEAD_DIM)
    v_s = proj_s[:, 2 * D_ATTN:3 * D_ATTN].reshape(nb_s, N_HEADS, HEAD_DIM)
    attn_s = _attn_sample(q_s, k_s, v_s, cache_k[l], cache_v[l]).reshape(nb_s, D_ATTN)
    k_new = _cache_update(cache_k[l], k_s)
    v_new = _cache_update(cache_v[l], v_s)

    abar, wb, wct = _ssm_prep(ssm_A_re[l], ssm_A_im[l], ssm_log_dt[l], ssm_B_re[l], ssm_B_im[l],
                              ssm_C_re[l], ssm_C_im[l])
    ssm_p, hre_p, him_p = _ssm_prompt(proj_p, nb_p, abar, wb, wct, ssm_d, wglu_bf, bglu, og)
    ssm_s, hre_s, him_s = _ssm_sample(proj_s[:, 3 * D_ATTN:], state_ssm_re[l].reshape(nb_s, N_STATE),
                                      state_ssm_im[l].reshape(nb_s, N_STATE), abar, wb, wct,
                                      ssm_d, wglu_bf, bglu, og)

    x1_p = _out_proj(attn_p, ssm_p, ag, w_out_bf, xp, mod_p, tm=tm_p, rows_per_mod=seq)
    x1_s = _out_proj(attn_s, ssm_s, ag, w_out_bf, xs, mod_s, tm=nb_s, rows_per_mod=nb_s)

    h2_p, te_p, gt_p = _route(x1_p, mod_p, ng2, wr_t_bf, b_router[l], tm=256, rows_per_mod=seq)
    h2_s, te_s, gt_s = _route(x1_s, mod_s, ng2, wr_t_bf, b_router[l], tm=nb_s, rows_per_mod=nb_s)
    t_pad = (t_p // DISP_TM + 1) * DISP_TM
    te = jnp.concatenate([te_p, te_s, jnp.full((TOP_K, t_pad - t_p - nb_s), -1, I32)], axis=1)
    dest, nblk, cnt, ps = _dispatch_tables(te)
    n_assign = (t_p + nb_s) * TOP_K
    n_slots = (-(-n_assign // MOE_BLK) + N_EXPERTS) * MOE_BLK
    xs_sorted = _scatter_tokens(dest, cnt, ps, nblk, h2_p, h2_s, n_slots)
    ys_sorted = _moe_experts(cnt, ps, nblk, xs_sorted, w_exp_gate[l], w_exp_up[l], w_exp_down[l])
    y_p = _combine(dest[:, :t_p], gt_p.T, h2_p, x1_p, mod_p, wsg_bf, wsu_bf, wsd_bf, ys_sorted,
                   tm=128, rows_per_mod=seq)
    y_s = _combine(dest[:, t_p:t_p + nb_s], gt_s.T, h2_s, x1_s, mod_s, wsg_bf, wsu_bf, wsd_bf, ys_sorted,
                   tm=nb_s, rows_per_mod=nb_s)

    keep = min(WBUF, seq)
    kv = proj_p.reshape(nb_p, seq, D_IN_PROJ)[:, seq - keep:, D_ATTN:3 * D_ATTN]
    k_p = kv[..., :D_ATTN].reshape(1, nb_p, keep, N_HEADS, HEAD_DIM)
    v_p = kv[..., D_ATTN:].reshape(1, nb_p, keep, N_HEADS, HEAD_DIM)
    st = lambda h, b: h.reshape(1, b, SSM_GROUPS, SSM_STATE)
    return (y_p.reshape(nb_p, seq, D_MODEL), y_s.reshape(nb_s, 1, D_MODEL), k_p, v_p,
            st(hre_p, nb_p), st(him_p, nb_p), k_new[None], v_new[None], st(hre_s, nb_s), st(him_s, nb_s))
```

```python
import functools
import math

import jax
import jax.numpy as jnp
from jax import lax
from jax.experimental import pallas as pl
from jax.experimental.pallas import tpu as pltpu

F32 = jnp.float32
BF16 = jnp.bfloat16
I32 = jnp.int32

D_MODEL = 2048
SEQ = 4096
PAST_LEN = 8192
D_ATTN = 1024
HEAD_DIM = 128
N_HEADS = 8
ROT_DIM = 32
ROPE_THETA = 500000.0
DILATIONS = (1, 4, 16)
SPAN = 128
WBUF = 2048
D_SSM = 1024
SSM_CH = 16
SSM_GROUPS = 64
SSM_STATE = 64
N_STATE = SSM_GROUPS * SSM_STATE
N_EXPERTS = 256
TOP_K = 8
N_EXPERT_GROUPS = 8
TOPK_GROUPS = 4
D_EXPERT = 512
ROUTE_SCALE = 2.5
EPS = 1e-6
D_IN_PROJ = 4096

MOE_BLK = 128
MOE_SHIFT = MOE_BLK.bit_length() - 1
LANES = 128
SUBLANES = 8
MIB = 1024 * 1024


def _cparams(sem, vmem_mib):
    return pltpu.CompilerParams(dimension_semantics=sem, vmem_limit_bytes=vmem_mib * MIB)


def _rms(x, g):
    return x * lax.rsqrt(jnp.mean(x * x, axis=-1, keepdims=True) + EPS) * g


def _dot(a, b):
    return jnp.dot(a, b, preferred_element_type=F32)


def _dot_nt(a, b):
    return lax.dot_general(a, b, (((1,), (1,)), ((), ())), preferred_element_type=F32)


def _ada_kernel(c_ref, w_ref, b_ref, o_ref):
    a = jax.nn.silu(c_ref[...]).astype(BF16)
    o_ref[...] = _dot(a, w_ref[...].astype(BF16)) + b_ref[...]


def _ada_mod(c_all, w_ada, b_ada):
    rows = c_all.shape[0]
    n_out = w_ada.shape[1]
    tn = 1024
    return pl.pallas_call(
        _ada_kernel,
        grid=(n_out // tn,),
        in_specs=[pl.BlockSpec((rows, D_MODEL), lambda j: (0, 0)),
                  pl.BlockSpec((D_MODEL, tn), lambda j: (0, j)),
                  pl.BlockSpec((1, tn), lambda j: (0, j))],
        out_specs=pl.BlockSpec((rows, tn), lambda j: (0, j)),
        out_shape=jax.ShapeDtypeStruct((rows, n_out), F32),
        compiler_params=_cparams(("arbitrary",), 40),
        name="ada_mod",
    )(c_all, w_ada, b_ada.reshape(1, n_out))


def _rope_kernel(c_ref, s1_ref, s2_ref, *, pos0, stride):
    n = c_ref.shape[0]
    lane = lax.broadcasted_iota(I32, (n, LANES), 1)
    row = lax.broadcasted_iota(I32, (n, LANES), 0) + pl.program_id(0) * n
    pos = (row * stride + pos0).astype(F32)
    half = ROT_DIM // 2
    fi = (lane & (half - 1)).astype(F32)
    inv = jnp.exp(-math.log(ROPE_THETA) * fi / half)
    ang = pos * inv
    cos = jnp.cos(ang)
    sin = jnp.sin(ang)
    c_ref[...] = jnp.where(lane < ROT_DIM, cos, 1.0)
    s1_ref[...] = jnp.where(lane < half, -sin, 0.0)
    s2_ref[...] = jnp.where((lane >= half) & (lane < ROT_DIM), sin, 0.0)


def _rope_tables(n_rows, pos0, stride):
    tr = min(n_rows, 512)
    spec = pl.BlockSpec((tr, LANES), lambda i: (i, 0))
    shp = jax.ShapeDtypeStruct((n_rows, LANES), F32)
    return pl.pallas_call(
        functools.partial(_rope_kernel, pos0=pos0, stride=stride),
        grid=(n_rows // tr,),
        in_specs=[],
        out_specs=[spec, spec, spec],
        out_shape=[shp, shp, shp],
        compiler_params=_cparams(("arbitrary",), 32),
        name="rope_tables",
    )()


def _inproj_kernel(x_ref, sc_ref, sh_ref, ng_ref, w_ref, qkg_ref, rc_ref, rs1_ref, rs2_ref,
                   o_ref, h_scr):
    n = pl.program_id(1)

    @pl.when(n == 0)
    def _():
        h = _rms(x_ref[...], ng_ref[...]) * (1.0 + sc_ref[...]) + sh_ref[...]
        h_scr[...] = h.astype(BF16)

    o_ref[...] = _dot(h_scr[...], w_ref[...])

    @pl.when(n < 2)
    def _():
        g = qkg_ref[n]
        c = rc_ref[...]
        s1 = rs1_ref[...]
        s2 = rs2_ref[...]
        for hh in range(N_HEADS):
            sl = slice(hh * HEAD_DIM, (hh + 1) * HEAD_DIM)
            y = _rms(o_ref[:, sl], g)
            o_ref[:, sl] = (y * c + pltpu.roll(y, HEAD_DIM - ROT_DIM // 2, 1) * s1
                            + pltpu.roll(y, ROT_DIM // 2, 1) * s2)


def _in_proj(x, mod, ng, w_bf, qkg, rope, *, tm, rows_per_mod, rope_blocks):
    t = x.shape[0]
    tn = 1024
    rm = mod.shape[1]
    mt = t // tm

    def mod_map(j):
        return lambda m, n: ((m * tm) // rows_per_mod, 0, j)

    rope_spec = pl.BlockSpec((tm, LANES), lambda m, n: (m % rope_blocks, 0))
    return pl.pallas_call(
        _inproj_kernel,
        grid=(mt, D_IN_PROJ // tn),
        in_specs=[pl.BlockSpec((tm, D_MODEL), lambda m, n: (m, 0)),
                  pl.BlockSpec((None, rm, D_MODEL), mod_map(1)),
                  pl.BlockSpec((None, rm, D_MODEL), mod_map(0)),
                  pl.BlockSpec((1, D_MODEL), lambda m, n: (0, 0)),
                  pl.BlockSpec((D_MODEL, tn), lambda m, n: (0, n)),
                  pl.BlockSpec((2, 1, HEAD_DIM), lambda m, n: (0, 0, 0)),
                  rope_spec, rope_spec, rope_spec],
        out_specs=pl.BlockSpec((tm, tn), lambda m, n: (m, n)),
        out_shape=jax.ShapeDtypeStruct((t, D_IN_PROJ), F32),
        scratch_shapes=[pltpu.VMEM((tm, D_MODEL), BF16)],
        compiler_params=_cparams(("arbitrary", "arbitrary"), 48),
        name="in_proj",
    )(x, mod, mod, ng, w_bf, qkg, *rope)


ATTN_GROUP = 2


def _attn_prompt_kernel(q_ref, k_ref, v_ref, o_ref, o0, o1, o2, l0, l1, l2):
    scale = 1.0 / math.sqrt(HEAD_DIM)
    qi = lax.broadcasted_iota(I32, (SPAN, 2 * SPAN), 0)
    kj = lax.broadcasted_iota(I32, (SPAN, 2 * SPAN), 1)
    is_prev = kj < SPAN
    prev_mask = is_prev & (kj >= qi)
    own_mask = (~is_prev) & (kj - SPAN <= qi)
    neg = -jnp.inf
    oaccs = (o0, o1, o2)
    laccs = (l0, l1, l2)

    for p, d in enumerate(DILATIONS):
        nbk = SEQ // (SPAN * d)
        shift = nbk.bit_length() - 1
        oacc = oaccs[p]
        lacc = laccs[p]

        def rows(start, d=d):
            return pl.ds(start, SPAN) if d == 1 else pl.ds(start, SPAN, stride=d)

        def body(it, carry, d=d, nbk=nbk, shift=shift, oacc=oacc, lacc=lacc, rows=rows):
            loaded = []
            for j in range(ATTN_GROUP):
                idx = it * ATTN_GROUP + j
                r = idx >> shift
                n = idx & (nbk - 1)
                start = r + n * (SPAN * d)
                pstart = jnp.maximum(start - SPAN * d, r)
                q = q_ref[rows(start), :]
                kk = jnp.concatenate([k_ref[rows(pstart), :], k_ref[rows(start), :]], axis=0)
                vv = jnp.concatenate([v_ref[rows(pstart), :], v_ref[rows(start), :]], axis=0)
                loaded.append((start, n, q, kk, vv))
            results = []
            for start, n, q, kk, vv in loaded:
                valid = own_mask | (prev_mask & (n > 0))
                s = jnp.where(valid, _dot_nt(q.astype(BF16), kk.astype(BF16)) * scale, neg)
                m = jnp.max(s, axis=-1, keepdims=True)
                pr = jnp.exp(s - m)
                den = jnp.sum(pr, axis=-1, keepdims=True)
                o = _dot(pr.astype(BF16), vv.astype(BF16)) / den
                results.append((start, o, m + jnp.log(den)))
            for start, o, lse in results:
                oacc[rows(start), :] = o
                lacc[rows(start), :] = jnp.broadcast_to(lse, (SPAN, HEAD_DIM))
            return carry

        lax.fori_loop(0, SEQ // SPAN // ATTN_GROUP, body, 0)

    ch = 512
    for c0 in range(0, SEQ, ch):
        sl = slice(c0, c0 + ch)
        la, lb, lc = l0[sl, :], l1[sl, :], l2[sl, :]
        mx = jnp.maximum(jnp.maximum(la, lb), lc)
        wa, wb, wc = jnp.exp(la - mx), jnp.exp(lb - mx), jnp.exp(lc - mx)
        o_ref[sl, :] = (wa * o0[sl, :] + wb * o1[sl, :] + wc * o2[sl, :]) / (wa + wb + wc)


def _attn_prompt(proj, batch):
    blk = (SEQ, HEAD_DIM)
    scr = pltpu.VMEM((SEQ, HEAD_DIM), F32)
    return pl.pallas_call(
        _attn_prompt_kernel,
        grid=(batch, N_HEADS),
        in_specs=[pl.BlockSpec(blk, lambda b, h: (b, h)),
                  pl.BlockSpec(blk, lambda b, h: (b, N_HEADS + h)),
                  pl.BlockSpec(blk, lambda b, h: (b, 2 * N_HEADS + h))],
        out_specs=pl.BlockSpec(blk, lambda b, h: (b, h)),
        out_shape=jax.ShapeDtypeStruct((batch * SEQ, D_ATTN), F32),
        scratch_shapes=[scr] * 6,
        compiler_params=_cparams(("arbitrary", "arbitrary"), 48),
        name="attn_prompt",
    )(proj, proj, proj)


def _attn_sample_kernel(q_ref, kn_ref, vn_ref, k1, k4, k16, v1, v4, v16, o_ref):
    scale = 1.0 / math.sqrt(HEAD_DIM)
    q = q_ref[...]
    kn = kn_ref[...]
    vn = vn_ref[...]
    s_new = jnp.sum(q * kn, axis=-1, keepdims=True) * scale
    outs = []
    lses = []
    for kc, vc in ((k1, v1), (k4, v4), (k16, v16)):
        kk = kc[...]
        vv = vc[...]
        s = jnp.sum(kk * q[None], axis=-1, keepdims=True) * scale
        m = jnp.maximum(jnp.max(s, axis=0), s_new)
        p = jnp.exp(s - m[None])
        p_new = jnp.exp(s_new - m)
        den = jnp.sum(p, axis=0) + p_new
        o = (jnp.sum(p * vv, axis=0) + p_new * vn) / den
        outs.append(o)
        lses.append(m + jnp.log(den))
    mx = jnp.maximum(jnp.maximum(lses[0], lses[1]), lses[2])
    ws = [jnp.exp(l - mx) for l in lses]
    o_ref[...] = (ws[0] * outs[0] + ws[1] * outs[1] + ws[2] * outs[2]) / (ws[0] + ws[1] + ws[2])


def _attn_sample(q, kn, vn, cache_k, cache_v):
    b = q.shape[0]
    row = pl.BlockSpec((None, N_HEADS, HEAD_DIM), lambda i: (i, 0, 0))
    views, specs = [], []
    for c in (cache_k, cache_v):
        for d in DILATIONS:
            views.append(c.reshape(b, WBUF // d, d, N_HEADS, HEAD_DIM))
            last = WBUF // d // SPAN - 1
            specs.append(pl.BlockSpec((None, SPAN, None, N_HEADS, HEAD_DIM),
                                      lambda i, last=last: (i, last, 0, 0, 0)))
    return pl.pallas_call(
        _attn_sample_kernel,
        grid=(b,),
        in_specs=[row, row, row] + specs,
        out_specs=row,
        out_shape=jax.ShapeDtypeStruct((b, N_HEADS, HEAD_DIM), F32),
        compiler_params=_cparams(("arbitrary",), 32),
        name="attn_sample",
    )(q, kn, vn, *views)


CACHE_CHUNK = 256


def _cache_kernel(c_ref, n_ref, o_ref):
    for r0 in range(0, WBUF - 1, CACHE_CHUNK):
        r1 = min(r0 + CACHE_CHUNK, WBUF - 1)
        o_ref[r0:r1] = c_ref[r0 + 1:r1 + 1]
    o_ref[WBUF - 1] = n_ref[...]


def _cache_update(cache, new):
    b = cache.shape[0]
    blk = pl.BlockSpec((None, WBUF, N_HEADS, HEAD_DIM), lambda i: (i, 0, 0, 0))
    return pl.pallas_call(
        _cache_kernel,
        grid=(b,),
        in_specs=[blk, pl.BlockSpec((None, N_HEADS, HEAD_DIM), lambda i: (i, 0, 0))],
        out_specs=blk,
        out_shape=jax.ShapeDtypeStruct(cache.shape, cache.dtype),
        compiler_params=_cparams(("arbitrary",), 40),
        name="cache_update",
    )(cache, new)


def _ssm_prep_kernel(ar_ref, ai_ref, ldt_ref, br_ref, bi_ref, cr_ref, ci_ref,
                     ab_ref, wb_ref, wc_ref):
    ar = ar_ref[...]
    ai = ai_ref[...]
    dt = jnp.exp(ldt_ref[...])
    mag = jnp.exp(dt * ar)
    abar_re = mag * jnp.cos(dt * ai)
    abar_im = mag * jnp.sin(dt * ai)
    nr = abar_re - 1.0
    den = ar * ar + ai * ai
    coef_re = (nr * ar + abar_im * ai) / den
    coef_im = (abar_im * ar - nr * ai) / den
    ab_ref[0] = abar_re
    ab_ref[1] = abar_im
    wb_ref[...] = jnp.zeros(wb_ref.shape, wb_ref.dtype)
    wc_ref[...] = jnp.zeros(wc_ref.shape, wc_ref.dtype)
    lane = lax.broadcasted_iota(I32, (SSM_CH, LANES), 1)
    gpt = 16
    for g in range(SSM_GROUPS):
        kt, gl = divmod(g, gpt)
        keep = (lane < SSM_STATE) if g % 2 == 0 else (lane >= SSM_STATE)
        cre = coef_re[g:g + 1, :]
        cim = coef_im[g:g + 1, :]
        bre = br_ref[g]
        bim = bi_ref[g]
        bb_re = jnp.where(keep, cre * bre - cim * bim, 0.0).astype(BF16)
        bb_im = jnp.where(keep, cre * bim + cim * bre, 0.0).astype(BF16)
        rs = slice(gl * SSM_CH, (gl + 1) * SSM_CH)
        lt = (gl // 2) * LANES
        wb_ref[kt, rs, lt:lt + LANES] = bb_re
        wb_ref[kt, rs, 1024 + lt:1024 + lt + LANES] = bb_im
        c_re = jnp.where(keep, cr_ref[g], 0.0).astype(BF16)
        c_im = jnp.where(keep, -ci_ref[g], 0.0).astype(BF16)
        wc_ref[kt, rs, lt:lt + LANES] = c_re
        wc_ref[kt, rs, 1024 + lt:1024 + lt + LANES] = c_im


def _ssm_prep(a_re, a_im, log_dt, b_re, b_im, c_re, c_im):
    dup = lambda x: jnp.concatenate([x, x], axis=-1)
    br_t = dup(jnp.swapaxes(b_re, 1, 2))
    bi_t = dup(jnp.swapaxes(b_im, 1, 2))
    shapes = [jax.ShapeDtypeStruct((2, SSM_GROUPS, LANES), F32),
              jax.ShapeDtypeStruct((4, 256, 2048), BF16),
              jax.ShapeDtypeStruct((4, 256, 2048), BF16)]
    ab, wb, wct = pl.pallas_call(
        _ssm_prep_kernel,
        out_shape=shapes,
        compiler_params=pltpu.CompilerParams(vmem_limit_bytes=40 * MIB),
        name="ssm_prep",
    )(dup(a_re), dup(a_im), log_dt.reshape(SSM_GROUPS, 1), br_t, bi_t, dup(c_re), dup(c_im))
    abar = ab[:, :, :SSM_STATE].reshape(2, N_STATE)
    return abar, wb, wct


def _ssm_bu(ub, wb_ref):
    return [_dot(ub[:, kt * 256:(kt + 1) * 256], wb_ref[kt]) for kt in range(4)]


def _ssm_tail(h_re_tiles, h_im_tiles, u, wc_ref, d_ref, wg_ref, bg_ref, og_ref):
    ys = []
    for kt in range(4):
        wct = wc_ref[kt]
        ys.append(_dot_nt(h_re_tiles[kt].astype(BF16), wct[:, :1024])
                  + _dot_nt(h_im_tiles[kt].astype(BF16), wct[:, 1024:]))
    y = jnp.concatenate(ys, axis=-1) + d_ref[...] * u
    g = jax.nn.gelu(y)
    z = _dot(g.astype(BF16), wg_ref[...]) + bg_ref[...]
    out = g * jax.nn.sigmoid(z)
    return _rms(out, og_ref[...])


SCAN_LANES = 512
SSM_TC = 256
SEG_LEN = SSM_TC // SUBLANES


def _split3(x):
    hi = x.astype(BF16)
    r1 = x - hi.astype(F32)
    mid = r1.astype(BF16)
    lo = (r1 - mid.astype(F32)).astype(BF16)
    return hi, mid, lo


def _ssm_prompt_kernel(u_ref, wb_ref, wc_ref, ab_ref, d_ref, wg_ref, bg_ref, og_ref,
                       o_ref, hre_ref, him_ref, bre, bim, cre, cim, pwr, pwi):
    c = pl.program_id(1)
    tc = SSM_TC
    lanes = SCAN_LANES

    def cmul(xr, xi, yr, yi):
        return xr * yr - xi * yi, xr * yi + xi * yr

    @pl.when(c == 0)
    def _():
        cre[...] = jnp.zeros(cre.shape, F32)
        cim[...] = jnp.zeros(cim.shape, F32)
        ar, ai = ab_ref[0:1, :], ab_ref[1:2, :]
        cur_r, cur_i = ar, ai
        for i in range(SEG_LEN):
            pwr[i * SUBLANES:(i + 1) * SUBLANES, :] = jnp.broadcast_to(cur_r, (SUBLANES, N_STATE))
            pwi[i * SUBLANES:(i + 1) * SUBLANES, :] = jnp.broadcast_to(cur_i, (SUBLANES, N_STATE))
            cur_r, cur_i = cmul(cur_r, cur_i, ar, ai)

    jj = lax.broadcasted_iota(I32, (tc, tc), 0)
    tt = lax.broadcasted_iota(I32, (tc, tc), 1)
    seg_shift = SEG_LEN.bit_length() - 1
    perm = (tt == ((jj & (SUBLANES - 1)) << seg_shift) + (jj >> 3)).astype(BF16)
    unperm = (tt == ((jj & (SEG_LEN - 1)) << 3) + (jj >> seg_shift)).astype(BF16)

    hi, mid, lo = _split3(u_ref[...])
    u_hi = _dot(perm, hi)
    u = u_hi + _dot(perm, mid) + _dot(perm, lo)
    bu = _ssm_bu(u_hi.astype(BF16), wb_ref)
    for kt in range(4):
        bre[:, kt * 1024:(kt + 1) * 1024] = bu[kt][:, :1024]
        bim[:, kt * 1024:(kt + 1) * 1024] = bu[kt][:, 1024:]

    row = lax.broadcasted_iota(I32, (SUBLANES, lanes), 0)
    zero = jnp.zeros((SUBLANES, lanes), F32)

    for lc in range(N_STATE // lanes):
        sl = slice(lc * lanes, (lc + 1) * lanes)
        ar = jnp.broadcast_to(ab_ref[0:1, sl], (SUBLANES, lanes))
        ai = jnp.broadcast_to(ab_ref[1:2, sl], (SUBLANES, lanes))

        def local_step(i, carry, sl=sl, ar=ar, ai=ai):
            xr, xi = carry
            off = pl.multiple_of(i * SUBLANES, SUBLANES)
            tr, ti = cmul(ar, ai, xr, xi)
            xr = tr + bre[pl.ds(off, SUBLANES), sl]
            xi = ti + bim[pl.ds(off, SUBLANES), sl]
            bre[pl.ds(off, SUBLANES), sl] = xr
            bim[pl.ds(off, SUBLANES), sl] = xi
            return xr, xi

        hr, hi_ = lax.fori_loop(0, SEG_LEN, local_step, (zero, zero), unroll=4)

        f1r = pwr[(SEG_LEN - 1) * SUBLANES:SEG_LEN * SUBLANES, sl]
        f1i = pwi[(SEG_LEN - 1) * SUBLANES:SEG_LEN * SUBLANES, sl]
        f2r, f2i = cmul(f1r, f1i, f1r, f1i)
        f4r, f4i = cmul(f2r, f2i, f2r, f2i)
        f8r, f8i = cmul(f4r, f4i, f4r, f4i)
        psr = jnp.ones((SUBLANES, lanes), F32)
        psi = zero
        for bit, (fr, fi) in ((1, (f1r, f1i)), (2, (f2r, f2i)), (4, (f4r, f4i)), (8, (f8r, f8i))):
            has = ((row + 1) & bit) != 0
            nr, ni = cmul(psr, psi, fr, fi)
            psr = jnp.where(has, nr, psr)
            psi = jnp.where(has, ni, psi)
        for s, fr, fi in ((1, f1r, f1i), (2, f2r, f2i), (4, f4r, f4i)):
            sr = jnp.where(row >= s, pltpu.roll(hr, s, 0), 0.0)
            si = jnp.where(row >= s, pltpu.roll(hi_, s, 0), 0.0)
            tr, ti = cmul(fr, fi, sr, si)
            hr = hr + tr
            hi_ = hi_ + ti
        cr, ci = cre[:, sl], cim[:, sl]
        tr, ti = cmul(psr, psi, cr, ci)
        hr = hr + tr
        hi_ = hi_ + ti
        pr = jnp.where(row == 0, cr, pltpu.roll(hr, 1, 0))
        pi = jnp.where(row == 0, ci, pltpu.roll(hi_, 1, 0))
        cre[:, sl] = jnp.broadcast_to(hr[SUBLANES - 1:SUBLANES, :], (SUBLANES, lanes))
        cim[:, sl] = jnp.broadcast_to(hi_[SUBLANES - 1:SUBLANES, :], (SUBLANES, lanes))

        def carry_step(i, carry, sl=sl, pr=pr, pi=pi):
            off = pl.multiple_of(i * SUBLANES, SUBLANES)
            tr, ti = cmul(pwr[pl.ds(off, SUBLANES), sl], pwi[pl.ds(off, SUBLANES), sl], pr, pi)
            bre[pl.ds(off, SUBLANES), sl] = bre[pl.ds(off, SUBLANES), sl] + tr
            bim[pl.ds(off, SUBLANES), sl] = bim[pl.ds(off, SUBLANES), sl] + ti
            return carry

        lax.fori_loop(0, SEG_LEN, carry_step, 0, unroll=4)

    h_re = [bre[:, kt * 1024:(kt + 1) * 1024] for kt in range(4)]
    h_im = [bim[:, kt * 1024:(kt + 1) * 1024] for kt in range(4)]
    out = _ssm_tail(h_re, h_im, u, wc_ref, d_ref, wg_ref, bg_ref, og_ref).astype(BF16)
    o_ref[...] = _dot(unperm, out).astype(BF16)

    @pl.when(c == pl.num_programs(1) - 1)
    def _():
        hre_ref[...] = cre[0:1, :]
        him_ref[...] = cim[0:1, :]


def _ssm_prompt(proj, batch, abar, wb, wct, ssm_d, wglu_bf, bglu, og):
    tc = SSM_TC
    nchunk = SEQ // tc
    const2 = lambda shape: pl.BlockSpec(shape, lambda b, c: (0,) * len(shape))
    st_spec = pl.BlockSpec((None, 1, N_STATE), lambda b, c: (b, 0, 0))
    st_shape = jax.ShapeDtypeStruct((batch, 1, N_STATE), F32)
    return pl.pallas_call(
        _ssm_prompt_kernel,
        grid=(batch, nchunk),
        in_specs=[pl.BlockSpec((tc, D_SSM), lambda b, c: (b * nchunk + c, 3)),
                  const2((4, 256, 2048)), const2((4, 256, 2048)), const2((2, N_STATE)),
                  const2((1, D_SSM)), const2((D_SSM, D_SSM)), const2((1, D_SSM)), const2((1, D_SSM))],
        out_specs=[pl.BlockSpec((tc, D_SSM), lambda b, c: (b * nchunk + c, 0)), st_spec, st_spec],
        out_shape=[jax.ShapeDtypeStruct((batch * SEQ, D_SSM), BF16), st_shape, st_shape],
        scratch_shapes=[pltpu.VMEM((tc, N_STATE), F32), pltpu.VMEM((tc, N_STATE), F32),
                        pltpu.VMEM((SUBLANES, N_STATE), F32), pltpu.VMEM((SUBLANES, N_STATE), F32),
                        pltpu.VMEM((tc, N_STATE), F32), pltpu.VMEM((tc, N_STATE), F32)],
        compiler_params=_cparams(("arbitrary", "arbitrary"), 56),
        name="ssm_prompt",
    )(proj, wb, wct, abar, ssm_d, wglu_bf, bglu, og)


def _ssm_sample_kernel(u_ref, h0r_ref, h0i_ref, wb_ref, wc_ref, ab_ref, d_ref, wg_ref, bg_ref, og_ref,
                       o_ref, hre_ref, him_ref):
    u = u_ref[...]
    bu = _ssm_bu(u.astype(BF16), wb_ref)
    h_re, h_im = [], []
    for kt in range(4):
        sl = slice(kt * 1024, (kt + 1) * 1024)
        ar = ab_ref[0:1, sl]
        ai = ab_ref[1:2, sl]
        h0r = h0r_ref[:, sl]
        h0i = h0i_ref[:, sl]
        hr = bu[kt][:, :1024] + (ar * h0r - ai * h0i)
        hi = bu[kt][:, 1024:] + (ar * h0i + ai * h0r)
        hre_ref[:, sl] = hr
        him_ref[:, sl] = hi
        h_re.append(hr)
        h_im.append(hi)
    o_ref[...] = _ssm_tail(h_re, h_im, u, wc_ref, d_ref, wg_ref, bg_ref, og_ref).astype(BF16)


def _ssm_sample(u, h0_re, h0_im, abar, wb, wct, ssm_d, wglu_bf, bglu, og):
    b = u.shape[0]
    st_shape = jax.ShapeDtypeStruct((b, N_STATE), F32)
    return pl.pallas_call(
        _ssm_sample_kernel,
        out_shape=[jax.ShapeDtypeStruct((b, D_SSM), BF16), st_shape, st_shape],
        compiler_params=pltpu.CompilerParams(vmem_limit_bytes=40 * MIB),
        name="ssm_sample",
    )(u, h0_re, h0_im, wb, wct, abar, ssm_d, wglu_bf, bglu, og)


def _outproj_kernel(attn_ref, ssm_ref, ag_ref, w_ref, x_ref, g1_ref, o_ref, lhs):
    n = pl.program_id(1)

    @pl.when(n == 0)
    def _():
        lhs[:, :D_ATTN] = _rms(attn_ref[...], ag_ref[...]).astype(BF16)
        lhs[:, D_ATTN:] = ssm_ref[...]

    o_ref[...] = x_ref[...] + g1_ref[...] * _dot(lhs[...], w_ref[...])


def _out_proj(attn, ssm_n, ag, w_bf, x, mod, *, tm, rows_per_mod):
    t = x.shape[0]
    tn = 1024
    rm = mod.shape[1]
    return pl.pallas_call(
        _outproj_kernel,
        grid=(t // tm, D_MODEL // tn),
        in_specs=[pl.BlockSpec((tm, D_ATTN), lambda m, n: (m, 0)),
                  pl.BlockSpec((tm, D_SSM), lambda m, n: (m, 0)),
                  pl.BlockSpec((1, D_ATTN), lambda m, n: (0, 0)),
                  pl.BlockSpec((D_MODEL, tn), lambda m, n: (0, n)),
                  pl.BlockSpec((tm, tn), lambda m, n: (m, n)),
                  pl.BlockSpec((None, rm, tn), lambda m, n: ((m * tm) // rows_per_mod, 0, 2 * (D_MODEL // tn) + n))],
        out_specs=pl.BlockSpec((tm, tn), lambda m, n: (m, n)),
        out_shape=jax.ShapeDtypeStruct((t, D_MODEL), F32),
        scratch_shapes=[pltpu.VMEM((tm, D_MODEL), BF16)],
        compiler_params=_cparams(("arbitrary", "arbitrary"), 48),
        name="out_proj",
    )(attn, ssm_n, ag, w_bf, x, mod)


def _route_kernel(x_ref, sc_ref, sh_ref, ng_ref, wr_ref, br_ref, h2_ref, te_ref, gt_ref):
    tm = x_ref.shape[0]
    h2 = _rms(x_ref[...], ng_ref[...]) * (1.0 + sc_ref[...]) + sh_ref[...]
    h2_ref[...] = h2
    scores = jax.nn.sigmoid(_dot_nt(wr_ref[...], h2.astype(BF16)))
    biased = scores + br_ref[...]
    neg = -jnp.inf
    per_g = N_EXPERTS // N_EXPERT_GROUPS
    gi_iota = lax.broadcasted_iota(I32, (per_g, tm), 0)
    gscore = []
    for g in range(N_EXPERT_GROUPS):
        xg = biased[g * per_g:(g + 1) * per_g, :]
        m1 = jnp.max(xg, axis=0, keepdims=True)
        i1 = jnp.min(jnp.where(xg == m1, gi_iota, per_g), axis=0, keepdims=True)
        m2 = jnp.max(jnp.where(gi_iota == i1, neg, xg), axis=0, keepdims=True)
        gscore.append(m1 + m2)
    parts = []
    for i in range(N_EXPERT_GROUPS):
        rank = jnp.zeros((1, tm), I32)
        for j in range(N_EXPERT_GROUPS):
            if j == i:
                continue
            ahead = (gscore[j] >= gscore[i]) if j < i else (gscore[j] > gscore[i])
            rank = rank + ahead.astype(I32)
        parts.append(jnp.where(rank < TOPK_GROUPS, biased[i * per_g:(i + 1) * per_g, :], neg))
    cur = jnp.concatenate(parts, axis=0)
    e_iota = lax.broadcasted_iota(I32, (N_EXPERTS, tm), 0)
    idxs, gates = [], []
    for _ in range(TOP_K):
        m = jnp.max(cur, axis=0, keepdims=True)
        idx = jnp.min(jnp.where(cur == m, e_iota, N_EXPERTS), axis=0, keepdims=True)
        hit = e_iota == idx
        gates.append(jnp.sum(jnp.where(hit, scores, 0.0), axis=0, keepdims=True))
        idxs.append(idx)
        cur = jnp.where(hit, neg, cur)
    tot = gates[0]
    for g in gates[1:]:
        tot = tot + g
    for k in range(TOP_K):
        te_ref[k:k + 1, :] = idxs[k]
        gt_ref[k:k + 1, :] = gates[k] / tot * ROUTE_SCALE


def _route(x1, mod, ng, wr_t_bf, b_router, *, tm, rows_per_mod):
    t = x1.shape[0]
    rm = mod.shape[1]

    def mod_map(j):
        return lambda m: ((m * tm) // rows_per_mod, 0, j)

    return pl.pallas_call(
        _route_kernel,
        grid=(t // tm,),
        in_specs=[pl.BlockSpec((tm, D_MODEL), lambda m: (m, 0)),
                  pl.BlockSpec((None, rm, D_MODEL), mod_map(4)),
                  pl.BlockSpec((None, rm, D_MODEL), mod_map(3)),
                  pl.BlockSpec((1, D_MODEL), lambda m: (0, 0)),
                  pl.BlockSpec((N_EXPERTS, D_MODEL), lambda m: (0, 0)),
                  pl.BlockSpec((N_EXPERTS, 1), lambda m: (0, 0))],
        out_specs=[pl.BlockSpec((tm, D_MODEL), lambda m: (m, 0)),
                   pl.BlockSpec((TOP_K, tm), lambda m: (0, m)),
                   pl.BlockSpec((TOP_K, tm), lambda m: (0, m))],
        out_shape=[jax.ShapeDtypeStruct((t, D_MODEL), F32),
                   jax.ShapeDtypeStruct((TOP_K, t), I32),
                   jax.ShapeDtypeStruct((TOP_K, t), F32)],
        compiler_params=_cparams(("arbitrary",), 40),
        name="route",
    )(x1, mod, mod, ng, wr_t_bf, b_router.reshape(N_EXPERTS, 1))


DISP_TM = 256


def _rank_kernel(te_ref, rank_ref, cnt_ref, run):
    i = pl.program_id(0)
    tm = te_ref.shape[1]

    @pl.when(i == 0)
    def _():
        run[...] = jnp.zeros(run.shape, F32)

    te = te_ref[...]
    e_iota = lax.broadcasted_iota(I32, (N_EXPERTS, tm), 0)
    a = jnp.zeros((N_EXPERTS, tm), F32)
    for k in range(TOP_K):
        a = a + (e_iota == te[k:k + 1, :]).astype(F32)
    upper = (lax.broadcasted_iota(I32, (tm, tm), 0) < lax.broadcasted_iota(I32, (tm, tm), 1)).astype(BF16)
    tot = _dot(a.astype(BF16), upper) + run[:, 0:1]
    for k in range(TOP_K):
        rk = jnp.sum(jnp.where(e_iota == te[k:k + 1, :], tot, 0.0), axis=0, keepdims=True)
        rank_ref[k:k + 1, :] = rk.astype(I32)
    run[...] = run[...] + jnp.sum(a, axis=1, keepdims=True)

    @pl.when(i == pl.num_programs(0) - 1)
    def _():
        cnt_ref[...] = run[...].astype(I32)


def _dest_kernel(te_ref, rank_ref, cnt_ref, dest_ref, ps_ref, nblk_ref, ps_scr):
    i = pl.program_id(0)
    tm = te_ref.shape[1]

    @pl.when(i == 0)
    def _():
        nb_e = ((cnt_ref[...] + (MOE_BLK - 1)) >> MOE_SHIFT).astype(F32)
        lower = (lax.broadcasted_iota(I32, (N_EXPERTS, N_EXPERTS), 1)
                 < lax.broadcasted_iota(I32, (N_EXPERTS, N_EXPERTS), 0)).astype(BF16)
        ps_b = _dot(lower, nb_e.astype(BF16))
        ps_scr[...] = ps_b * MOE_BLK
        ps_ref[...] = (ps_b * MOE_BLK).astype(I32)
        nblk_ref[...] = jnp.max(ps_b + nb_e, axis=0, keepdims=True).astype(I32)

    te = te_ref[...]
    rk = rank_ref[...]
    ps = ps_scr[:, 0:1]
    e_iota = lax.broadcasted_iota(I32, (N_EXPERTS, tm), 0)
    for k in range(TOP_K):
        tek = te[k:k + 1, :]
        base = jnp.sum(jnp.where(e_iota == tek, ps, 0.0), axis=0, keepdims=True)
        dest_ref[k:k + 1, :] = jnp.where(tek >= 0, base.astype(I32) + rk[k:k + 1, :], -1)


def _dispatch_tables(te):
    tpad = te.shape[1]
    tm = DISP_TM
    tile = pl.BlockSpec((TOP_K, tm), lambda i: (0, i))
    full = lambda shape: pl.BlockSpec(shape, lambda i: (0,) * len(shape))
    rank, cnt = pl.pallas_call(
        _rank_kernel,
        grid=(tpad // tm,),
        in_specs=[tile],
        out_specs=[tile, full((N_EXPERTS, LANES))],
        out_shape=[jax.ShapeDtypeStruct((TOP_K, tpad), I32),
                   jax.ShapeDtypeStruct((N_EXPERTS, LANES), I32)],
        scratch_shapes=[pltpu.VMEM((N_EXPERTS, LANES), F32)],
        compiler_params=_cparams(("arbitrary",), 32),
        name="moe_rank",
    )(te)
    dest, ps, nblk = pl.pallas_call(
        _dest_kernel,
        grid=(tpad // tm,),
        in_specs=[tile, tile, full((N_EXPERTS, LANES))],
        out_specs=[tile, full((N_EXPERTS, LANES)), full((1, LANES))],
        out_shape=[jax.ShapeDtypeStruct((TOP_K, tpad), I32),
                   jax.ShapeDtypeStruct((N_EXPERTS, LANES), I32),
                   jax.ShapeDtypeStruct((1, LANES), I32)],
        scratch_shapes=[pltpu.VMEM((N_EXPERTS, LANES), F32)],
        compiler_params=_cparams(("arbitrary",), 32),
        name="moe_dest",
    )(te, rank, cnt)
    return dest, nblk[0, :1], cnt[:, 0], ps[:, 0]


PAD_BITS = (64, 32, 16, 8)


def _scatter_kernel(cnt_ref, ps_ref, nblk_ref, dest_ref, hp_ref, hs_ref, xs_ref, zeros, sem, zsem, *,
                    n_prompt_tiles):
    i = pl.program_id(0)
    tm = dest_ref.shape[1]

    def row_copy(src_ref, r, d):
        return pltpu.make_async_copy(src_ref.at[pl.ds(r, 1)], xs_ref.at[pl.ds(d, 1)], sem)

    def scatter_rows(src_ref, n_rows):
        def issue(r, c):
            for k in range(TOP_K):
                row_copy(src_ref, r, dest_ref[k, r]).start()
            return c

        def drain(r, c):
            for k in range(TOP_K):
                row_copy(src_ref, r, dest_ref[k, r]).wait()
            return c

        lax.fori_loop(0, n_rows, issue, 0, unroll=2)
        lax.fori_loop(0, n_rows, drain, 0, unroll=2)

    @pl.when(i < n_prompt_tiles)
    def _():
        scatter_rows(hp_ref, tm)

    @pl.when(i == n_prompt_tiles)
    def _():
        scatter_rows(hs_ref, hs_ref.shape[0])

    @pl.when(i == 0)
    def _():
        zeros[...] = jnp.zeros(zeros.shape, F32)

        def pad_copies(e, do):
            n = cnt_ref[e]
            npad = (((n + (MOE_BLK - 1)) >> MOE_SHIFT) << MOE_SHIFT) - n
            base = ps_ref[e] + n
            head = npad & (SUBLANES - 1)
            for s in range(SUBLANES - 1):
                @pl.when(s < head)
                def _(s=s):
                    do(pltpu.make_async_copy(zeros.at[pl.ds(0, 1)], xs_ref.at[pl.ds(base + s, 1)], zsem))
            off = base + head
            for bit in PAD_BITS:
                take = (npad & bit) != 0

                @pl.when(take)
                def _(off=off, bit=bit):
                    dst = xs_ref.at[pl.ds(pl.multiple_of(off, SUBLANES), bit)]
                    do(pltpu.make_async_copy(zeros.at[pl.ds(0, bit)], dst, zsem))
                off = off + jnp.where(take, bit, 0)

        def issue(e, c):
            pad_copies(e, lambda cp: cp.start())
            return c

        def drain(e, c):
            pad_copies(e, lambda cp: cp.wait())
            return c

        lax.fori_loop(0, N_EXPERTS, issue, 0)
        lax.fori_loop(0, N_EXPERTS, drain, 0)

        def tail_copy(b):
            dst = xs_ref.at[pl.ds(pl.multiple_of(b * MOE_BLK, MOE_BLK), MOE_BLK)]
            return pltpu.make_async_copy(zeros, dst, zsem)

        def tail_issue(b, c):
            @pl.when(b >= nblk_ref[0])
            def _():
                tail_copy(b).start()
            return c

        def tail_drain(b, c):
            @pl.when(b >= nblk_ref[0])
            def _():
                tail_copy(b).wait()
            return c

        n_blocks = xs_ref.shape[0] // MOE_BLK
        lax.fori_loop(0, n_blocks, tail_issue, 0)
        lax.fori_loop(0, n_blocks, tail_drain, 0)


def _scatter_tokens(dest, cnt, ps, nblk, h2_p, h2_s, n_slots):
    tm = DISP_TM
    n_prompt_tiles = h2_p.shape[0] // tm
    grid_spec = pltpu.PrefetchScalarGridSpec(
        num_scalar_prefetch=3,
        grid=(n_prompt_tiles + 1,),
        in_specs=[pl.BlockSpec((TOP_K, tm), lambda i, *_: (0, i), memory_space=pltpu.SMEM),
                  pl.BlockSpec((tm, D_MODEL), lambda i, *_: (jnp.minimum(i, n_prompt_tiles - 1), 0)),
                  pl.BlockSpec(h2_s.shape, lambda i, *_: (0, 0))],
        out_specs=pl.BlockSpec(memory_space=pl.ANY),
        scratch_shapes=[pltpu.VMEM((MOE_BLK, D_MODEL), F32),
                        pltpu.SemaphoreType.DMA(()), pltpu.SemaphoreType.DMA(())],
    )
    return pl.pallas_call(
        functools.partial(_scatter_kernel, n_prompt_tiles=n_prompt_tiles),
        grid_spec=grid_spec,
        out_shape=jax.ShapeDtypeStruct((n_slots, D_MODEL), F32),
        compiler_params=_cparams(("arbitrary",), 32),
        name="moe_scatter",
    )(cnt, ps, nblk, dest, h2_p, h2_s)


ROW_DMA_PRIORITY = 1
X_BUFS = 4
X_AHEAD = X_BUFS - 1


def _moe_kernel(cnt_ref, ps_ref, nblk_ref, wg_ref, wu_ref, wd_ref, xs_ref, ys_ref,
                wgb, wub, wdb, xbuf, ybuf, xsem, ysem):
    e = pl.program_id(0)
    total = nblk_ref[0]
    nb = (cnt_ref[e] + (MOE_BLK - 1)) >> MOE_SHIFT
    g0 = ps_ref[e] >> MOE_SHIFT

    def rows(g):
        return pl.ds(pl.multiple_of(g * MOE_BLK, MOE_BLK), MOE_BLK)

    def x_copy(g, slot):
        return pltpu.make_async_copy(xs_ref.at[rows(g)], xbuf.at[slot], xsem.at[slot])

    def y_copy(g, slot):
        return pltpu.make_async_copy(ybuf.at[slot], ys_ref.at[rows(g)], ysem.at[slot])

    @pl.when(e == 0)
    def _():
        for j in range(X_AHEAD):
            @pl.when(j < total)
            def _(j=j):
                x_copy(j, j).start(priority=ROW_DMA_PRIORITY)

    @pl.when(nb > 0)
    def _():
        wgb[...] = wg_ref[...].astype(BF16)
        wub[...] = wu_ref[...].astype(BF16)
        wdb[...] = wd_ref[...].astype(BF16)

        def body(c, carry):
            g = g0 + c
            slot = g & 1
            xslot = g & (X_BUFS - 1)

            @pl.when(g + X_AHEAD < total)
            def _():
                x_copy(g + X_AHEAD, (g + X_AHEAD) & (X_BUFS - 1)).start(priority=ROW_DMA_PRIORITY)

            x_copy(g, xslot).wait()

            @pl.when(g >= 2)
            def _():
                y_copy(g - 2, slot).wait()

            x = xbuf[xslot].astype(BF16)
            hid = jax.nn.silu(_dot(x, wgb[...])) * _dot(x, wub[...])
            ybuf[slot] = _dot(hid.astype(BF16), wdb[...])
            y_copy(g, slot).start(priority=ROW_DMA_PRIORITY)
            return carry

        lax.fori_loop(0, nb, body, 0)

    @pl.when(e == pl.num_programs(0) - 1)
    def _():
        @pl.when(total >= 2)
        def _():
            y_copy(total - 2, total & 1).wait()

        y_copy(total - 1, (total - 1) & 1).wait()

        ybuf[0] = jnp.zeros((MOE_BLK, D_MODEL), F32)

        def tail_issue(b, c):
            @pl.when(b >= total)
            def _():
                y_copy(b, 0).start()
            return c

        def tail_drain(b, c):
            @pl.when(b >= total)
            def _():
                y_copy(b, 0).wait()
            return c

        n_blocks = ys_ref.shape[0] // MOE_BLK
        lax.fori_loop(0, n_blocks, tail_issue, 0)
        lax.fori_loop(0, n_blocks, tail_drain, 0)


def _moe_experts(cnt, ps, nblk, xs, w_gate, w_up, w_down):
    n_slots = xs.shape[0]
    w_map = lambda e, *_: (e, 0, 0)
    anyspec = pl.BlockSpec(memory_space=pl.ANY)
    grid_spec = pltpu.PrefetchScalarGridSpec(
        num_scalar_prefetch=3,
        grid=(N_EXPERTS,),
        in_specs=[pl.BlockSpec((None, D_MODEL, D_EXPERT), w_map),
                  pl.BlockSpec((None, D_MODEL, D_EXPERT), w_map),
                  pl.BlockSpec((None, D_EXPERT, D_MODEL), w_map),
                  anyspec],
        out_specs=anyspec,
        scratch_shapes=[pltpu.VMEM((D_MODEL, D_EXPERT), BF16), pltpu.VMEM((D_MODEL, D_EXPERT), BF16),
                        pltpu.VMEM((D_EXPERT, D_MODEL), BF16),
                        pltpu.VMEM((X_BUFS, MOE_BLK, D_MODEL), F32), pltpu.VMEM((2, MOE_BLK, D_MODEL), F32),
                        pltpu.SemaphoreType.DMA((X_BUFS,)), pltpu.SemaphoreType.DMA((2,))],
    )
    return pl.pallas_call(
        _moe_kernel,
        grid_spec=grid_spec,
        out_shape=jax.ShapeDtypeStruct((n_slots, D_MODEL), F32),
        compiler_params=_cparams(("arbitrary",), 48),
        name="moe_experts",
    )(cnt, ps, nblk, w_gate, w_up, w_down, xs)


def _combine_kernel(dcur_ref, dnxt_ref, gate_ref, h2_ref, x1_ref, g2_ref, wsg_ref, wsu_ref, wsd_ref,
                    ys_ref, o_ref, ybuf, sem):
    i = pl.program_id(0)
    n = pl.num_programs(0)
    tm = h2_ref.shape[0]
    slot = i % 2

    def row_copy(d, s, k, r):
        return pltpu.make_async_copy(ys_ref.at[pl.ds(d, 1)], ybuf.at[s, k, pl.ds(r, 1)], sem.at[s])

    def gather(dref, s, wait):
        def step(r, c):
            for k in range(TOP_K):
                cp = row_copy(dref[k, r], s, k, r)
                if wait:
                    cp.wait()
                else:
                    cp.start()
            return c

        lax.fori_loop(0, tm, step, 0, unroll=2)

    @pl.when(i == 0)
    def _():
        gather(dcur_ref, 0, False)

    for par in (0, 1):
        @pl.when((slot == par) & (i + 1 < n))
        def _(par=par):
            gather(dnxt_ref, 1 - par, False)

    for par in (0, 1):
        @pl.when(slot == par)
        def _(par=par):
            gather(dcur_ref, par, True)

    gates = gate_ref[...]
    routed = gates[:, 0:1] * ybuf[slot, 0]
    for k in range(1, TOP_K):
        routed = routed + gates[:, k:k + 1] * ybuf[slot, k]
    hb = h2_ref[...].astype(BF16)
    hid = jax.nn.silu(_dot(hb, wsg_ref[...])) * _dot(hb, wsu_ref[...])
    shared = _dot(hid.astype(BF16), wsd_ref[...])
    o_ref[...] = x1_ref[...] + g2_ref[...] * (routed + shared)


def _combine(dest, gate_tk, h2, x1, mod, wsg_bf, wsu_bf, wsd_bf, ys, *, tm, rows_per_mod):
    t = h2.shape[0]
    nt = t // tm
    rm = mod.shape[1]
    const = lambda shape: pl.BlockSpec(shape, lambda i: (0,) * len(shape))
    return pl.pallas_call(
        _combine_kernel,
        grid=(nt,),
        in_specs=[pl.BlockSpec((TOP_K, tm), lambda i: (0, i), memory_space=pltpu.SMEM),
                  pl.BlockSpec((TOP_K, tm), lambda i: (0, jnp.minimum(i + 1, nt - 1)), memory_space=pltpu.SMEM),
                  pl.BlockSpec((tm, TOP_K), lambda i: (i, 0)),
                  pl.BlockSpec((tm, D_MODEL), lambda i: (i, 0)),
                  pl.BlockSpec((tm, D_MODEL), lambda i: (i, 0)),
                  pl.BlockSpec((None, rm, D_MODEL), lambda i: ((i * tm) // rows_per_mod, 0, 5)),
                  const((D_MODEL, D_EXPERT)), const((D_MODEL, D_EXPERT)), const((D_EXPERT, D_MODEL)),
                  pl.BlockSpec(memory_space=pl.ANY)],
        out_specs=pl.BlockSpec((tm, D_MODEL), lambda i: (i, 0)),
        out_shape=jax.ShapeDtypeStruct((t, D_MODEL), F32),
        scratch_shapes=[pltpu.VMEM((2, TOP_K, tm, D_MODEL), F32), pltpu.SemaphoreType.DMA((2,))],
        compiler_params=_cparams(("arbitrary",), 48),
        name="moe_combine",
    )(dest, dest, gate_tk, h2, x1, mod, wsg_bf, wsu_bf, wsd_bf, ys)


def kernel(x_prompt, x_sample, cache_k, cache_v, state_ssm_re, state_ssm_im, c_prompt, c_sample,
           w_ada, b_ada, norm1_g, w_in, q_norm_g, k_norm_g, ssm_A_re, ssm_A_im, ssm_log_dt,
           ssm_B_re, ssm_B_im, ssm_C_re, ssm_C_im, ssm_D, ssm_w_glu, ssm_b_glu, attn_out_g,
           ssm_out_g, w_out, norm2_g, w_router, b_router, w_exp_gate, w_exp_up, w_exp_down,
           w_sh_gate, w_sh_up, w_sh_down):
    nb_p, seq, _ = x_prompt.shape
    nb_s = x_sample.shape[0]
    t_p = nb_p * seq
    l = 0

    n_c = nb_p + nb_s
    c_rows = -(-n_c // SUBLANES) * SUBLANES
    c_all = jnp.concatenate([c_prompt, c_sample, jnp.zeros((c_rows - n_c, D_MODEL), F32)], axis=0)
    mod = _ada_mod(c_all, w_ada[l], b_ada[l])
    mod_p = mod[:nb_p].reshape(nb_p, 1, 6 * D_MODEL)
    mod_s = mod[nb_p:n_c].reshape(1, nb_s, 6 * D_MODEL)

    w_in_bf = w_in[l].astype(BF16)
    w_out_bf = w_out[l].astype(BF16)
    wglu_bf = ssm_w_glu[l].astype(BF16)
    wr_t_bf = w_router[l].T.astype(BF16)
    wsg_bf = w_sh_gate[l].astype(BF16)
    wsu_bf = w_sh_up[l].astype(BF16)
    wsd_bf = w_sh_down[l].astype(BF16)
    ng1 = norm1_g[l].reshape(1, D_MODEL)
    ng2 = norm2_g[l].reshape(1, D_MODEL)
    qkg = jnp.stack([q_norm_g[l], k_norm_g[l]]).reshape(2, 1, HEAD_DIM)
    ag = attn_out_g[l].reshape(1, D_ATTN)
    og = ssm_out_g[l].reshape(1, D_SSM)
    ssm_d = ssm_D[l].reshape(1, D_SSM)
    bglu = ssm_b_glu[l].reshape(1, D_SSM)

    xp = x_prompt.reshape(t_p, D_MODEL)
    xs = x_sample.reshape(nb_s, D_MODEL)

    tm_p = 512
    rope_p = _rope_tables(seq, 0, 1)
    rope_s = _rope_tables(nb_s, PAST_LEN, 0)
    proj_p = _in_proj(xp, mod_p, ng1, w_in_bf, qkg, rope_p, tm=tm_p, rows_per_mod=seq,
                      rope_blocks=seq // tm_p)
    proj_s = _in_proj(xs, mod_s, ng1, w_in_bf, qkg, rope_s, tm=nb_s, rows_per_mod=nb_s, rope_blocks=1)

    attn_p = _attn_prompt(proj_p, nb_p)
    q_s = proj_s[:, :D_ATTN].reshape(nb_s, N_HEADS, HEAD_DIM)
    k_s = proj_s[:, D_ATTN:2 * D_ATTN].reshape(nb_s, N_HEADS, HEAD_DIM)
    v_s = proj_s[:, 2 * D_ATTN:3 * D_ATTN].reshape(nb_s, N_HEADS, HEAD_DIM)
    attn_s = _attn_sample(q_s, k_s, v_s, cache_k[l], cache_v[l]).reshape(nb_s, D_ATTN)
    k_new = _cache_update(cache_k[l], k_s)
    v_new = _cache_update(cache_v[l], v_s)

    abar, wb, wct = _ssm_prep(ssm_A_re[l], ssm_A_im[l], ssm_log_dt[l], ssm_B_re[l], ssm_B_im[l],
                              ssm_C_re[l], ssm_C_im[l])
    ssm_p, hre_p, him_p = _ssm_prompt(proj_p, nb_p, abar, wb, wct, ssm_d, wglu_bf, bglu, og)
    ssm_s, hre_s, him_s = _ssm_sample(proj_s[:, 3 * D_ATTN:], state_ssm_re[l].reshape(nb_s, N_STATE),
                                      state_ssm_im[l].reshape(nb_s, N_STATE), abar, wb, wct,
                                      ssm_d, wglu_bf, bglu, og)

    x1_p = _out_proj(attn_p, ssm_p, ag, w_out_bf, xp, mod_p, tm=tm_p, rows_per_mod=seq)
    x1_s = _out_proj(attn_s, ssm_s, ag, w_out_bf, xs, mod_s, tm=nb_s, rows_per_mod=nb_s)

    h2_p, te_p, gt_p = _route(x1_p, mod_p, ng2, wr_t_bf, b_router[l], tm=256, rows_per_mod=seq)
    h2_s, te_s, gt_s = _route(x1_s, mod_s, ng2, wr_t_bf, b_router[l], tm=nb_s, rows_per_mod=nb_s)
    t_pad = (t_p // DISP_TM + 1) * DISP_TM
    te = jnp.concatenate([te_p, te_s, jnp.full((TOP_K, t_pad - t_p - nb_s), -1, I32)], axis=1)
    dest, nblk, cnt, ps = _dispatch_tables(te)
    n_assign = (t_p + nb_s) * TOP_K
    n_slots = (-(-n_assign // MOE_BLK) + N_EXPERTS) * MOE_BLK
    xs_sorted = _scatter_tokens(dest, cnt, ps, nblk, h2_p, h2_s, n_slots)
    ys_sorted = _moe_experts(cnt, ps, nblk, xs_sorted, w_exp_gate[l], w_exp_up[l], w_exp_down[l])
    y_p = _combine(dest[:, :t_p], gt_p.T, h2_p, x1_p, mod_p, wsg_bf, wsu_bf, wsd_bf, ys_sorted,
                   tm=128, rows_per_mod=seq)
    y_s = _combine(dest[:, t_p:t_p + nb_s], gt_s.T, h2_s, x1_s, mod_s, wsg_bf, wsu_bf, wsd_bf, ys_sorted,
                   tm=nb_s, rows_per_mod=nb_s)

    keep = min(WBUF, seq)
    kv = proj_p.reshape(nb_p, seq, D_IN_PROJ)[:, seq - keep:, D_ATTN:3 * D_ATTN]
    k_p = kv[..., :D_ATTN].reshape(1, nb_p, keep, N_HEADS, HEAD_DIM)
    v_p = kv[..., D_ATTN:].reshape(1, nb_p, keep, N_HEADS, HEAD_DIM)
    st = lambda h, b: h.reshape(1, b, SSM_GROUPS, SSM_STATE)
    return (y_p.reshape(nb_p, seq, D_MODEL), y_s.reshape(nb_s, 1, D_MODEL), k_p, v_p,
            st(hre_p, nb_p), st(him_p, nb_p), k_new[None], v_new[None], st(hre_s, nb_s), st(him_s, nb_s))
```
